```python
import math
import jax, jax.numpy as jnp
from jax import lax
import numpy as np

D_MODEL = 1024
BATCH = 2
SEQ = 16384
DEPTH = 1
DEC_BATCH = 128
DEC_SEQ = 8
PAST_LEN = 8192
PAGE_SIZE = 128

HEAD_DIM = 64
A_HEADS = 12
B_HEADS = 4
B_DK = 64
B_DV = 64
A_WIDTH = A_HEADS * HEAD_DIM
B_WIDTH = B_HEADS * B_DV
D_MIX = A_WIDTH + B_WIDTH
DILATED_BRANCHES = ((128, 1), (512, 4), (2048, 16))
W_MAX = 2048
BAND = 128
ATTN_SCALE = HEAD_DIM ** -0.5
ROPE_DIM = HEAD_DIM // 4
ROPE_THETA = 500000.0
CONV_W = 4
CONV_DIM = 2 * B_HEADS * B_DK + B_HEADS * B_DV
CHUNK = 64
N_EXPERTS = 32
TOP_K = 4
D_FF = D_MODEL
SWIGLU_LIMIT = 7.0
SWIGLU_ALPHA = 1.702
MOE_BLOCK = 128
DN_ALPHA = (2.0 * DEPTH) ** 0.25
DN_BETA = (8.0 * DEPTH) ** -0.25
N_IN = 3 * A_WIDTH + CONV_DIM + B_WIDTH + 2 * B_HEADS
LN_EPS = 1e-5
NORM_EPS = 1e-6

kernel_name = 'hybrid_dilated_swa_gdn_moe_step'


def layer_norm(x, g, b):
    xf = x.astype(jnp.float32)
    mu = xf.mean(-1, keepdims=True)
    var = jnp.square(xf - mu).mean(-1, keepdims=True)
    return ((xf - mu) * lax.rsqrt(var + LN_EPS) * g + b).astype(x.dtype)


def rope(x, pos):
    half = ROPE_DIM // 2
    inv = ROPE_THETA ** (-jnp.arange(half, dtype=jnp.float32) / half)
    ang = pos.astype(jnp.float32)[:, None] * inv[None, :]
    cos = jnp.cos(ang)[None, :, None, :]
    sin = jnp.sin(ang)[None, :, None, :]
    xr = x[..., :ROPE_DIM].astype(jnp.float32)
    x1, x2 = xr[..., :half], xr[..., half:]
    rot = jnp.concatenate([x1 * cos - x2 * sin, x2 * cos + x1 * sin], axis=-1).astype(x.dtype)
    return jnp.concatenate([rot, x[..., ROPE_DIM:]], axis=-1)


def dilated_band_attention(q, k, v, dilation, window):
    b, s_len, h, dh = q.shape
    n_cls = s_len // dilation
    wc = window // dilation
    nb = -(-n_cls // BAND)
    lp = nb * BAND
    z = b * dilation

    def to_classes(t, front):
        t = t.reshape(b, n_cls, dilation, h, dh).transpose(0, 2, 1, 3, 4).reshape(z, n_cls, h, dh)
        return jnp.pad(t, ((0, 0), (front, lp - n_cls), (0, 0), (0, 0)))

    qb = to_classes(q, 0).reshape(z, nb, BAND, h, dh)
    kb = to_classes(k, BAND).reshape(z, nb + 1, BAND, h, dh)
    vb = to_classes(v, BAND).reshape(z, nb + 1, BAND, h, dh)
    kband = jnp.concatenate([kb[:, :-1], kb[:, 1:]], axis=2)
    vband = jnp.concatenate([vb[:, :-1], vb[:, 1:]], axis=2)
    s = jnp.einsum('znqhd,znkhd->znhqk', qb, kband, preferred_element_type=jnp.float32) * ATTN_SCALE
    qi = jnp.arange(BAND)[:, None]
    ki = jnp.arange(2 * BAND)[None, :]
    dist = qi + BAND - ki
    key_cls = (jnp.arange(nb)[:, None, None] - 1) * BAND + ki[None]
    valid = (dist >= 0)[None] & (dist <= wc)[None] & (key_cls >= 0)
    s = jnp.where(valid[None, :, None], s, -jnp.inf)
    m = s.max(-1, keepdims=True)
    p = jnp.exp(s - m)
    den = p.sum(-1, keepdims=True)
    o = jnp.einsum('znhqk,znkhd->znqhd', p / den, vband, preferred_element_type=jnp.float32)
    lse = (m + jnp.log(den))[..., 0]
    o = o.reshape(z, lp, h, dh)[:, :n_cls].reshape(b, dilation, n_cls, h, dh)
    o = o.transpose(0, 2, 1, 3, 4).reshape(b, s_len, h, dh)
    lse = lse.transpose(0, 1, 3, 2).reshape(z, lp, h)[:, :n_cls].reshape(b, dilation, n_cls, h)
    lse = lse.transpose(0, 2, 1, 3).reshape(b, s_len, h)
    return o, lse


def dilated_gather_attention(q, k_all, v_all, buf_len, dilation, window):
    t_new = q.shape[1]
    wc = window // dilation
    idx = buf_len + jnp.arange(t_new)[:, None] - dilation * jnp.arange(wc + 1)[None, :]
    valid = idx >= 0
    idx = jnp.maximum(idx, 0)
    kg = jnp.take(k_all, idx, axis=1)
    vg = jnp.take(v_all, idx, axis=1)
    s = jnp.einsum('bthd,btjhd->bthj', q, kg, preferred_element_type=jnp.float32) * ATTN_SCALE
    s = jnp.where(valid[None, :, None, :], s, -jnp.inf)
    m = s.max(-1, keepdims=True)
    p = jnp.exp(s - m)
    den = p.sum(-1, keepdims=True)
    o = jnp.einsum('bthj,btjhd->bthd', p / den, vg, preferred_element_type=jnp.float32)
    return o, (m + jnp.log(den))[..., 0]


def combine_branches(outs, lses):
    o = jnp.stack(outs, 0)
    wts = jax.nn.softmax(jnp.stack(lses, 0), axis=0)
    return jnp.einsum('rbshd,rbsh->bshd', o, wts)


def causal_conv(x, buf, w):
    t_len = x.shape[1]
    xp = jnp.concatenate([buf.astype(x.dtype), x], axis=1)
    y = xp[:, 0:t_len] * w[0]
    for j in range(1, CONV_W):
        y = y + xp[:, j:j + t_len] * w[j]
    return jax.nn.silu(y), xp[:, -(CONV_W - 1):]


def l2norm(x):
    return x * lax.rsqrt(jnp.sum(x * x, axis=-1, keepdims=True) + NORM_EPS)


def gated_delta_chunked(q, k, v, g, beta, s0):
    bn, t_len, h, dk = q.shape
    dv = v.shape[-1]
    n = -(-t_len // CHUNK)
    pad = n * CHUNK - t_len

    def prep4(t):
        t = jnp.pad(t, ((0, 0), (0, pad), (0, 0), (0, 0)))
        return t.reshape(bn, n, CHUNK, h, t.shape[-1]).transpose(1, 0, 3, 2, 4)

    def prep3(t):
        t = jnp.pad(t, ((0, 0), (0, pad), (0, 0)))
        return t.reshape(bn, n, CHUNK, h).transpose(1, 0, 3, 2)

    qc, kc, vc = prep4(q), prep4(k), prep4(v)
    gc, bc = prep3(g), prep3(beta)
    gcum = jnp.cumsum(gc, axis=-1)
    tri_incl = jnp.tril(jnp.ones((CHUNK, CHUNK), dtype=bool))
    tri_strict = jnp.tril(jnp.ones((CHUNK, CHUNK), dtype=bool), -1)
    decay = jnp.exp(jnp.where(tri_incl, gcum[..., :, None] - gcum[..., None, :], -jnp.inf))
    kb = kc * bc[..., None]
    lmat = jnp.where(tri_strict, jnp.einsum('nbhid,nbhjd->nbhij', kb, kc) * decay, 0.0)
    eye = jnp.eye(CHUNK, dtype=jnp.float32)
    a_mat = eye + lmat
    tinv = lax.linalg.triangular_solve(a_mat, jnp.broadcast_to(eye, a_mat.shape), left_side=True, lower=True)
    u = tinv @ (vc * bc[..., None])
    w = tinv @ (kb * jnp.exp(gcum)[..., None])
    attn = jnp.einsum('nbhid,nbhjd->nbhij', qc, kc) * decay
    qg = qc * jnp.exp(gcum)[..., None]
    kdec = kc * jnp.exp(gcum[..., -1:] - gcum)[..., None]
    glast = jnp.exp(gcum[..., -1])

    def step(state, xs):
        u_i, w_i, attn_i, qg_i, kdec_i, gl_i = xs
        v_new = u_i - w_i @ state
        o_i = qg_i @ state + attn_i @ v_new
        state = state * gl_i[..., None, None] + jnp.einsum('bhcd,bhce->bhde', kdec_i, v_new)
        return state, o_i

    s_fin, o = lax.scan(step, s0, (u, w, attn, qg, kdec, glast))
    o = o.transpose(1, 0, 3, 2, 4).reshape(bn, n * CHUNK, h, dv)[:, :t_len]
    return o, s_fin


def mixer_b(qkv_b, z, a_in, b_in, conv_buf, s0, conv_w, a_log, dt_bias, norm_w, out_dtype):
    conv_out, new_buf = causal_conv(qkv_b, conv_buf, conv_w)
    bn, t_len, _ = conv_out.shape
    cf = conv_out.astype(jnp.float32)
    q = cf[..., :B_HEADS * B_DK].reshape(bn, t_len, B_HEADS, B_DK)
    k = cf[..., B_HEADS * B_DK:2 * B_HEADS * B_DK].reshape(bn, t_len, B_HEADS, B_DK)
    v = cf[..., 2 * B_HEADS * B_DK:].reshape(bn, t_len, B_HEADS, B_DV)
    q = l2norm(q) * (B_DK ** -0.5)
    k = l2norm(k)
    beta = jax.nn.sigmoid(b_in.astype(jnp.float32))
    g = -jnp.exp(a_log.astype(jnp.float32)) * jax.nn.softplus(a_in.astype(jnp.float32) + dt_bias)
    o, s_new = gated_delta_chunked(q, k, v, g, beta, s0.astype(jnp.float32))
    o = o * lax.rsqrt(jnp.mean(o * o, axis=-1, keepdims=True) + NORM_EPS) * norm_w
    o = o * jax.nn.silu(z.astype(jnp.float32).reshape(bn, t_len, B_HEADS, B_DV))
    return o.reshape(bn, t_len, B_WIDTH).astype(out_dtype), new_buf, s_new.astype(s0.dtype)


def expert_block(xb, e, w_gu, b_gu, w_down, b_down):
    hcat = xb @ w_gu[e] + b_gu[e]
    gate = jnp.minimum(hcat[:, :D_FF], SWIGLU_LIMIT)
    up = jnp.clip(hcat[:, D_FF:], -SWIGLU_LIMIT, SWIGLU_LIMIT)
    glu = gate * jax.nn.sigmoid(SWIGLU_ALPHA * gate)
    return (glu * (up + 1.0)) @ w_down[e] + b_down[e]


def moe(h, router_w, router_b, w_gu, b_gu, w_down, b_down):
    shp = h.shape
    x = h.reshape(-1, D_MODEL)
    n_tok = x.shape[0]
    n_assign = n_tok * TOP_K
    logits = (x @ router_w).astype(jnp.float32) + router_b
    top_v, top_i = lax.top_k(logits, TOP_K)
    wts = jax.nn.softmax(top_v, axis=-1)
    eid = top_i.reshape(-1)
    tok = jnp.arange(n_assign, dtype=jnp.int32) // TOP_K
    order = jnp.argsort(eid)
    e_s, tok_s, w_s = eid[order], tok[order], wts.reshape(-1)[order]
    counts = jnp.zeros((N_EXPERTS,), jnp.int32).at[eid].add(1)
    start = jnp.cumsum(counts) - counts
    padded = (counts + MOE_BLOCK - 1) // MOE_BLOCK * MOE_BLOCK
    pstart = jnp.cumsum(padded) - padded
    pend = pstart + padded
    dest = pstart[e_s] + jnp.arange(n_assign, dtype=jnp.int32) - start[e_s]
    n_blocks = -(-(n_assign + N_EXPERTS * (MOE_BLOCK - 1)) // MOE_BLOCK)
    cap = n_blocks * MOE_BLOCK
    row_tok = jnp.full((cap,), n_tok, jnp.int32).at[dest].set(tok_s)
    row_w = jnp.zeros((cap,), jnp.float32).at[dest].set(w_s)
    xpad = jnp.concatenate([x, jnp.zeros((1, D_MODEL), x.dtype)], axis=0)
    xs = xpad[row_tok].reshape(n_blocks, MOE_BLOCK, D_MODEL)
    blk_e = jnp.minimum(jnp.searchsorted(pend, jnp.arange(n_blocks) * MOE_BLOCK, side='right'), N_EXPERTS - 1)
    ys = lax.map(lambda args: expert_block(args[0], args[1], w_gu, b_gu, w_down, b_down), (xs, blk_e))
    out = jnp.zeros((n_tok + 1, D_MODEL), jnp.float32).at[row_tok].add(ys.reshape(cap, D_MODEL) * row_w[:, None])
    return out[:n_tok].astype(h.dtype).reshape(shp)


def layer(x, c, pos, k_buf, v_buf, conv_buf, ssm0,
          w_ada, b_ada, w_in, conv_w, a_log, dt_bias, gdn_norm_w, w_out, ln1_g, ln1_b,
          router_w, router_b, w_gu, b_gu, w_down, b_down, ln2_g, ln2_b):
    bn, t_len, _ = x.shape
    mod = jax.nn.silu(c) @ w_ada + b_ada
    sh1, sc1, g1, sh2, sc2, g2 = [m[:, None, :] for m in jnp.split(mod, 6, axis=-1)]
    h = x * (1.0 + sc1) + sh1
    proj = h @ w_in
    o1 = 3 * A_WIDTH
    o2 = o1 + CONV_DIM
    o3 = o2 + B_WIDTH
    qa = rope(proj[..., :A_WIDTH].reshape(bn, t_len, A_HEADS, HEAD_DIM), pos)
    ka = rope(proj[..., A_WIDTH:2 * A_WIDTH].reshape(bn, t_len, A_HEADS, HEAD_DIM), pos)
    va = proj[..., 2 * A_WIDTH:o1].reshape(bn, t_len, A_HEADS, HEAD_DIM)
    outs, lses = [], []
    if k_buf is None:
        for window, dilation in DILATED_BRANCHES:
            o_r, l_r = dilated_band_attention(qa, ka, va, dilation, window)
            outs.append(o_r)
            lses.append(l_r)
        keep = min(W_MAX, t_len)
        new_k, new_v = ka[:, t_len - keep:], va[:, t_len - keep:]
    else:
        buf_len = k_buf.shape[1]
        k_all = jnp.concatenate([k_buf.astype(ka.dtype), ka], axis=1)
        v_all = jnp.concatenate([v_buf.astype(va.dtype), va], axis=1)
        for window, dilation in DILATED_BRANCHES:
            o_r, l_r = dilated_gather_attention(qa, k_all, v_all, buf_len, dilation, window)
            outs.append(o_r)
            lses.append(l_r)
        new_k, new_v = k_all[:, t_len:], v_all[:, t_len:]
    oa = combine_branches(outs, lses).astype(x.dtype).reshape(bn, t_len, A_WIDTH)
    ob, new_conv, new_ssm = mixer_b(proj[..., o1:o2], proj[..., o2:o3], proj[..., o3:o3 + B_HEADS],
                                    proj[..., o3 + B_HEADS:], conv_buf, ssm0, conv_w, a_log, dt_bias,
                                    gdn_norm_w, x.dtype)
    mix = jnp.concatenate([oa, ob], axis=-1) @ w_out
    x = layer_norm(DN_ALPHA * x + (1.0 + g1) * mix, ln1_g, ln1_b)
    h = x * (1.0 + sc2) + sh2
    x = layer_norm(DN_ALPHA * x + (1.0 + g2) * moe(h, router_w, router_b, w_gu, b_gu, w_down, b_down), ln2_g, ln2_b)
    return x, new_k, new_v, new_conv, new_ssm


def setup_inputs(seed: int = 0) -> dict:
    key = jax.random.key(seed)
    ks = jax.random.split(key, 28)

    def nrm(k, shape, scale):
        return jax.random.normal(k, shape, jnp.float32) * scale

    a_buf = min(W_MAX, PAST_LEN)
    dt = jnp.exp(jax.random.uniform(ks[10], (DEPTH, B_HEADS), jnp.float32, math.log(1e-3), math.log(1e-1)))
    return {
        'x_prompt': nrm(ks[0], (BATCH, SEQ, D_MODEL), 1.0),
        'x_sample': nrm(ks[1], (DEC_BATCH, DEC_SEQ, D_MODEL), 1.0),
        'c_prompt': nrm(ks[2], (BATCH, D_MODEL), 1.0),
        'c_sample': nrm(ks[3], (DEC_BATCH, D_MODEL), 1.0),
        'cache_a_k': nrm(ks[4], (DEPTH, DEC_BATCH, a_buf, A_HEADS, HEAD_DIM), 1.0),
        'cache_a_v': nrm(ks[5], (DEPTH, DEC_BATCH, a_buf, A_HEADS, HEAD_DIM), 1.0),
        'state_b_conv': nrm(ks[6], (DEPTH, DEC_BATCH, CONV_W - 1, CONV_DIM), 1.0),
        'state_b_ssm': nrm(ks[7], (DEPTH, DEC_BATCH, B_HEADS, B_DK, B_DV), 0.5),
        'w_ada': nrm(ks[8], (DEPTH, D_MODEL, 6 * D_MODEL), 0.1 * D_MODEL ** -0.5),
        'b_ada': nrm(ks[9], (DEPTH, 6 * D_MODEL), 0.01),
        'w_in': nrm(ks[11], (DEPTH, D_MODEL, N_IN), D_MODEL ** -0.5),
        'conv_w': nrm(ks[12], (DEPTH, CONV_W, CONV_DIM), CONV_W ** -0.5),
        'a_log': jnp.log(jax.random.uniform(ks[13], (DEPTH, B_HEADS), jnp.float32, 1.0, 16.0)),
        'dt_bias': dt + jnp.log(-jnp.expm1(-dt)),
        'gdn_norm_w': 1.0 + nrm(ks[14], (DEPTH, B_DV), 0.01),
        'w_out': nrm(ks[15], (DEPTH, D_MIX, D_MODEL), DN_BETA * D_MIX ** -0.5),
        'ln1_g': 1.0 + nrm(ks[16], (DEPTH, D_MODEL), 0.01),
        'ln1_b': nrm(ks[17], (DEPTH, D_MODEL), 0.01),
        'router_w': nrm(ks[18], (DEPTH, D_MODEL, N_EXPERTS), D_MODEL ** -0.5),
        'router_b': nrm(ks[19], (DEPTH, N_EXPERTS), 0.01),
        'w_gu': nrm(ks[20], (DEPTH, N_EXPERTS, D_MODEL, 2 * D_FF), D_MODEL ** -0.5),
        'b_gu': nrm(ks[21], (DEPTH, N_EXPERTS, 2 * D_FF), 0.01),
        'w_down': nrm(ks[22], (DEPTH, N_EXPERTS, D_FF, D_MODEL), DN_BETA * D_FF ** -0.5),
        'b_down': nrm(ks[23], (DEPTH, N_EXPERTS, D_MODEL), 0.01),
        'ln2_g': 1.0 + nrm(ks[24], (DEPTH, D_MODEL), 0.01),
        'ln2_b': nrm(ks[25], (DEPTH, D_MODEL), 0.01),
    }


def reference(x_prompt, x_sample, c_prompt, c_sample, cache_a_k, cache_a_v, state_b_conv, state_b_ssm,
              w_ada, b_ada, w_in, conv_w, a_log, dt_bias, gdn_norm_w, w_out, ln1_g, ln1_b,
              router_w, router_b, w_gu, b_gu, w_down, b_down, ln2_g, ln2_b):
    bp, tp, _ = x_prompt.shape
    pos_prompt = jnp.arange(tp, dtype=jnp.int32)
    pos_sample = PAST_LEN + jnp.arange(x_sample.shape[1], dtype=jnp.int32)
    conv0 = jnp.zeros((bp, CONV_W - 1, CONV_DIM), x_prompt.dtype)
    ssm0 = jnp.zeros((bp, B_HEADS, B_DK, B_DV), jnp.float32)
    yp, ys = x_prompt, x_sample
    kp, vp, cp, sp, kq, vq, cq, sq = [], [], [], [], [], [], [], []
    for l in range(DEPTH):
        wts = (w_ada[l], b_ada[l], w_in[l], conv_w[l], a_log[l], dt_bias[l], gdn_norm_w[l], w_out[l],
               ln1_g[l], ln1_b[l], router_w[l], router_b[l], w_gu[l], b_gu[l], w_down[l], b_down[l],
               ln2_g[l], ln2_b[l])
        yp, nk, nv, nc, ns = layer(yp, c_prompt, pos_prompt, None, None, conv0, ssm0, *wts)
        kp.append(nk)
        vp.append(nv)
        cp.append(nc)
        sp.append(ns)
        ys, nk, nv, nc, ns = layer(ys, c_sample, pos_sample, cache_a_k[l], cache_a_v[l],
                                   state_b_conv[l], state_b_ssm[l], *wts)
        kq.append(nk)
        vq.append(nv)
        cq.append(nc)
        sq.append(ns)
    return (yp, ys, jnp.stack(kp), jnp.stack(vp), jnp.stack(cp), jnp.stack(sp),
            jnp.stack(kq), jnp.stack(vq), jnp.stack(cq), jnp.stack(sq))
```

```python
import functools
import math

import jax
import jax.numpy as jnp
from jax import lax
from jax.experimental import pallas as pl
from jax.experimental.pallas import tpu as pltpu

F32 = jnp.float32
BF16 = jnp.bfloat16

D_MODEL = 1024
HEAD_DIM = 64
A_HEADS = 12
B_HEADS = 4
B_DK = 64
B_DV = 64
A_WIDTH = A_HEADS * HEAD_DIM
B_WIDTH = B_HEADS * B_DV
DILATED_BRANCHES = ((128, 1), (512, 4), (2048, 16))
W_MAX = 2048
BAND = 128
ATTN_SCALE = HEAD_DIM ** -0.5
ROPE_DIM = HEAD_DIM // 4
ROPE_THETA = 500000.0
CONV_W = 4
CONV_DIM = 2 * B_HEADS * B_DK + B_HEADS * B_DV
N_EXPERTS = 32
TOP_K = 4
D_FF = D_MODEL
SWIGLU_LIMIT = 7.0
SWIGLU_ALPHA = 1.702
LN_EPS = 1e-5
NORM_EPS = 1e-6
N_IN = 3 * A_WIDTH + CONV_DIM + B_WIDTH + 2 * B_HEADS

LANES = 128
SUBLANES = 8
N_IN_PAD = -(-N_IN // LANES) * LANES
GDN_CHUNK = 64
MOE_TM = 512
NEG = -1e30
VMEM_LIMIT = 56 * 1024 * 1024


def _dot(a, b):
    return jnp.dot(a, b, preferred_element_type=F32)


def _dot_nt(a, b):
    return lax.dot_general(a, b, (((1,), (1,)), ((), ())), preferred_element_type=F32)


def _dot_tn(a, b):
    return lax.dot_general(a, b, (((0,), (0,)), ((), ())), preferred_element_type=F32)


def _split2(x):
    hi = x.astype(BF16)
    lo = (x - hi.astype(F32)).astype(BF16)
    return hi, lo


def _split3(x):
    hi = x.astype(BF16)
    r = x - hi.astype(F32)
    mid = r.astype(BF16)
    lo = (r - mid.astype(F32)).astype(BF16)
    return hi, mid, lo


def _dot3(a, b, dot=_dot):
    ah, al = _split2(a)
    bh, bl = _split2(b)
    return dot(ah, bh) + (dot(ah, bl) + dot(al, bh))


def _dot_sel_rhs(a, sel_bf16):
    hi, mid, lo = _split3(a)
    return _dot(hi, sel_bf16) + (_dot(mid, sel_bf16) + _dot(lo, sel_bf16))


def _dot_sel_lhs(sel_bf16, b):
    hi, mid, lo = _split3(b)
    return _dot(sel_bf16, hi) + (_dot(sel_bf16, mid) + _dot(sel_bf16, lo))


def _sigmoid(x):
    return 1.0 / (1.0 + jnp.exp(-x))


def _softplus(x):
    return jnp.maximum(x, 0.0) + jnp.log(1.0 + jnp.exp(-jnp.abs(x)))


def _layer_norm(y, g, b):
    mu = jnp.mean(y, axis=-1, keepdims=True)
    yc = y - mu
    var = jnp.mean(yc * yc, axis=-1, keepdims=True)
    return yc * lax.rsqrt(var + LN_EPS) * g + b


def _ada_body(c_ref, w_ref, b_ref, o_ref):
    c = c_ref[...]
    s = c * _sigmoid(c)
    o_ref[...] = _dot3(s, w_ref[...]) + b_ref[...]


def _ada_mod(c, w_ada, b_ada):
    n = c.shape[0]
    n_pad = -(-n // SUBLANES) * SUBLANES
    c = jnp.pad(c, ((0, n_pad - n), (0, 0)))
    tn = 512
    out = pl.pallas_call(
        _ada_body,
        grid=(w_ada.shape[1] // tn,),
        in_specs=[pl.BlockSpec((n_pad, D_MODEL), lambda j: (0, 0)),
                  pl.BlockSpec((D_MODEL, tn), lambda j: (0, j)),
                  pl.BlockSpec((1, tn), lambda j: (0, j))],
        out_specs=pl.BlockSpec((n_pad, tn), lambda j: (0, j)),
        out_shape=jax.ShapeDtypeStruct((n_pad, w_ada.shape[1]), F32),
        name="ada_mod",
    )(c, w_ada, b_ada.reshape(1, -1))
    return out[:n]


def _rope_tables(pos):
    half = ROPE_DIM // 2
    inv = ROPE_THETA ** (-jnp.arange(half, dtype=F32) / half)
    ang = pos.astype(F32)[:, None] * inv[None, :]
    cos, sin = jnp.cos(ang), jnp.sin(ang)
    t = pos.shape[0]
    pad = jnp.zeros((t, HEAD_DIM - ROPE_DIM), F32)
    zero = jnp.zeros((t, half), F32)
    ct = jnp.concatenate([cos, cos, pad + 1.0], axis=1)
    s1 = jnp.concatenate([-sin, zero, pad], axis=1)
    s2 = jnp.concatenate([zero, sin, pad], axis=1)
    rep = LANES // HEAD_DIM
    return tuple(jnp.tile(a, (1, rep)) for a in (ct, s1, s2))


def _inproj_body(x_ref, sc_ref, sh_ref, ct_ref, s1_ref, s2_ref, w_ref,
                 q_ref, k_ref, v_ref, c_ref, z_ref, ab_ref):
    h = x_ref[...] * (1.0 + sc_ref[...]) + sh_ref[...]
    hb = h.astype(BF16)
    ct, s1, s2 = ct_ref[...], s1_ref[...], s2_ref[...]

    def rope(t):
        return t * ct + pltpu.roll(t, LANES - ROPE_DIM // 2, 1) * s1 + pltpu.roll(t, ROPE_DIM // 2, 1) * s2

    for j in range(A_WIDTH // LANES):
        cols = slice(LANES * j, LANES * (j + 1))
        q_ref[:, cols] = rope(_dot(hb, w_ref[:, LANES * j:LANES * (j + 1)])) * ATTN_SCALE
        k_ref[:, cols] = rope(_dot(hb, w_ref[:, A_WIDTH + LANES * j:A_WIDTH + LANES * (j + 1)]))
    o1 = 3 * A_WIDTH
    o2 = o1 + CONV_DIM
    o3 = o2 + B_WIDTH
    v_ref[...] = _dot(hb, w_ref[:, 2 * A_WIDTH:o1])
    c_ref[...] = _dot(hb, w_ref[:, o1:o2])
    z_ref[...] = _dot(hb, w_ref[:, o2:o3])
    ab_ref[...] = _dot(hb, w_ref[:, o3:N_IN_PAD])


def _in_proj(x, sc, sh, tables, w_pad, tm):
    b, t, _ = x.shape
    per_row = sc.shape[1] != 1
    mod_spec = (pl.BlockSpec((None, tm, D_MODEL), lambda bi, i: (bi, i, 0)) if per_row
                else pl.BlockSpec((None, 1, D_MODEL), lambda bi, i: (bi, 0, 0)))
    tab_spec = pl.BlockSpec((tm, LANES), lambda bi, i: (i, 0))

    def out(width):
        return (pl.BlockSpec((None, tm, width), lambda bi, i: (bi, i, 0)),
                jax.ShapeDtypeStruct((b, t, width), F32))

    outs = [out(A_WIDTH), out(A_WIDTH), out(A_WIDTH), out(CONV_DIM), out(B_WIDTH), out(LANES)]
    return pl.pallas_call(
        _inproj_body,
        grid=(b, t // tm),
        in_specs=[pl.BlockSpec((None, tm, D_MODEL), lambda bi, i: (bi, i, 0)), mod_spec, mod_spec,
                  tab_spec, tab_spec, tab_spec,
                  pl.BlockSpec((D_MODEL, N_IN_PAD), lambda bi, i: (0, 0))],
        out_specs=[o[0] for o in outs],
        out_shape=[o[1] for o in outs],
        compiler_params=pltpu.CompilerParams(vmem_limit_bytes=VMEM_LIMIT),
        name="in_proj",
    )(x, sc, sh, *tables, w_pad)


ATTN_TQ = 2048


def _attn_body(q_ref, kc_ref, kp_ref, vc_ref, vp_ref, o_ref, kbuf, vbuf, acc, mst, lst, bias):
    tq = ATTN_TQ
    first_tile = pl.program_id(1) == 0
    kbuf[0:tq, :] = kp_ref[...]
    kbuf[tq:2 * tq, :] = kc_ref[...]
    vbuf[0:tq, :] = vp_ref[...]
    vbuf[tq:2 * tq, :] = vc_ref[...]

    qi = lax.broadcasted_iota(jnp.int32, (BAND, 2 * BAND), 0)
    ki = lax.broadcasted_iota(jnp.int32, (BAND, 2 * BAND), 1)
    dist = qi + BAND - ki
    bias[0] = jnp.where((dist >= 0) & (dist <= BAND), 0.0, NEG)
    bias[1] = jnp.where(ki >= BAND, 0.0, NEG)
    lo = lax.broadcasted_iota(jnp.int32, (BAND, LANES), 1) < HEAD_DIM

    for r, (window, d) in enumerate(DILATED_BRANCHES):
        assert window // d == BAND
        span = BAND * d

        def blk(idx, carry, r=r, d=d, span=span):
            c = idx % d
            j = idx // d
            qs = j * span + c
            ks = tq + qs - span
            if d == 1:
                qs = pl.multiple_of(qs, BAND)
                ks = pl.multiple_of(ks, BAND)
                q_idx, k_idx = pl.ds(qs, BAND), pl.ds(ks, 2 * BAND)
            else:
                q_idx, k_idx = pl.ds(qs, BAND, stride=d), pl.ds(ks, 2 * BAND, stride=d)
            qb = q_ref[q_idx, :]
            kb = kbuf[k_idx, :].astype(BF16)
            vb = vbuf[k_idx, :].astype(BF16)
            no_prev = jnp.where(first_tile & (j == 0), 1.0, 0.0)
            mask = bias[0] + bias[1] * no_prev
            res = []
            for hsel in (lo, jnp.logical_not(lo)):
                qh = jnp.where(hsel, qb, 0.0).astype(BF16)
                s = _dot_nt(qh, kb) + mask
                m = jnp.max(s, axis=-1, keepdims=True)
                p = jnp.exp(s - m)
                l = jnp.sum(p, axis=-1, keepdims=True)
                res.append((_dot(p.astype(BF16), vb), m, l))
            (o0, m0, l0), (o1, m1, l1) = res
            acc[r, q_idx, :] = jnp.where(lo, o0, o1)
            mst[r, q_idx, :] = jnp.where(lo, m0, m1)
            lst[r, q_idx, :] = jnp.where(lo, l0, l1)
            return carry

        lax.fori_loop(0, tq // BAND, blk, 0)

    def combine(t, carry):
        rows = pl.ds(pl.multiple_of(t * BAND, BAND), BAND)
        ms = [mst[r, rows, :] for r in range(3)]
        m = jnp.maximum(jnp.maximum(ms[0], ms[1]), ms[2])
        ws = [jnp.exp(mr - m) for mr in ms]
        num = ws[0] * acc[0, rows, :] + ws[1] * acc[1, rows, :] + ws[2] * acc[2, rows, :]
        den = ws[0] * lst[0, rows, :] + ws[1] * lst[1, rows, :] + ws[2] * lst[2, rows, :]
        o_ref[rows, :] = (num / den).astype(o_ref.dtype)
        return carry

    lax.fori_loop(0, tq // BAND, combine, 0)


def _attn_prompt(q, k, v):
    b, t, _ = q.shape
    tq = ATTN_TQ
    assert t % tq == 0
    cur = pl.BlockSpec((None, tq, LANES), lambda bi, i, hp: (bi, i, hp))
    prev = pl.BlockSpec((None, tq, LANES), lambda bi, i, hp: (bi, jnp.maximum(i - 1, 0), hp))
    return pl.pallas_call(
        _attn_body,
        grid=(b, t // tq, A_WIDTH // LANES),
        in_specs=[cur, cur, prev, cur, prev],
        out_specs=cur,
        out_shape=jax.ShapeDtypeStruct((b, t, A_WIDTH), BF16),
        scratch_shapes=[pltpu.VMEM((2 * tq, LANES), F32), pltpu.VMEM((2 * tq, LANES), F32),
                        pltpu.VMEM((3, tq, LANES), F32), pltpu.VMEM((3, tq, LANES), F32),
                        pltpu.VMEM((3, tq, LANES), F32), pltpu.VMEM((2, BAND, 2 * BAND), F32)],
        compiler_params=pltpu.CompilerParams(vmem_limit_bytes=VMEM_LIMIT),
        name="attn_prompt",
    )(q, k, k, v, v)


def _attn_sample_body(q_ref, kn_ref, vn_ref, ck_hbm, cv_hbm, o_ref, ok_hbm, ov_hbm,
                      kbuf, vbuf, in_sem, out_sem, *, a_buf, t_new):
    b = pl.program_id(0)
    nb = pl.num_programs(0)
    slot = b % 2
    other = 1 - slot

    ncol = A_WIDTH // LANES

    def in_copies(bi, s):
        cps = []
        for j in range(ncol):
            cols = pl.ds(j * LANES, LANES)
            cps.append(pltpu.make_async_copy(ck_hbm.at[bi, :, cols], kbuf.at[s, j, pl.ds(0, a_buf)], in_sem.at[0, s]))
            cps.append(pltpu.make_async_copy(cv_hbm.at[bi, :, cols], vbuf.at[s, j, pl.ds(0, a_buf)], in_sem.at[1, s]))
        return cps

    def out_copies(bi, s):
        cps = []
        for j in range(ncol):
            cols = pl.ds(j * LANES, LANES)
            cps.append(pltpu.make_async_copy(kbuf.at[s, j, pl.ds(t_new, a_buf)], ok_hbm.at[bi, :, cols], out_sem.at[0, s]))
            cps.append(pltpu.make_async_copy(vbuf.at[s, j, pl.ds(t_new, a_buf)], ov_hbm.at[bi, :, cols], out_sem.at[1, s]))
        return cps

    @pl.when(b == 0)
    def _():
        for cp in in_copies(0, 0):
            cp.start()

    @pl.when(b > 0)
    def _():
        for cp in out_copies(b - 1, other):
            cp.wait()

    @pl.when(b + 1 < nb)
    def _():
        for cp in in_copies(b + 1, other):
            cp.start()

    for cp in in_copies(b, slot):
        cp.wait()
    for j in range(ncol):
        kbuf[slot, j, pl.ds(a_buf, t_new), :] = kn_ref[:, j * LANES:(j + 1) * LANES]
        vbuf[slot, j, pl.ds(a_buf, t_new), :] = vn_ref[:, j * LANES:(j + 1) * LANES]
    for cp in out_copies(b, slot):
        cp.start()

    half = lax.broadcasted_iota(jnp.int32, (LANES, LANES), 0) // HEAD_DIM
    col = lax.broadcasted_iota(jnp.int32, (LANES, LANES), 1)
    half_t = lax.broadcasted_iota(jnp.int32, (LANES, LANES), 1) // HEAD_DIM
    col_t = lax.broadcasted_iota(jnp.int32, (LANES, LANES), 0)
    heads_per_col = LANES // HEAD_DIM
    ind = [jnp.where(col == half + heads_per_col * j, 1.0, 0.0).astype(BF16) for j in range(ncol)]
    ind_t = [jnp.where(col_t == half_t + heads_per_col * j, 1.0, 0.0).astype(BF16) for j in range(ncol)]

    def seg_scores(rows, q_cols):
        s = None
        for j in range(ncol):
            hi, lo = _split2(kbuf[slot, j, rows, :] * q_cols[j])
            sj = _dot(hi, ind[j]) + _dot(lo, ind[j])
            s = sj if s is None else s + sj
        return s

    def spread(p, j):
        hi, lo = _split2(p)
        return _dot(hi, ind_t[j]) + _dot(lo, ind_t[j])

    q_all = q_ref[...]
    row8 = lax.broadcasted_iota(jnp.int32, (t_new, LANES), 0)
    rowb = lax.broadcasted_iota(jnp.int32, (BAND, LANES), 0)
    d4 = DILATED_BRANCHES[1][1]
    d16 = DILATED_BRANCHES[2][1]
    for t in range(t_new):
        q_cols = [q_all[t:t + 1, j * LANES:(j + 1) * LANES] for j in range(ncol)]
        delta = t - row8
        mult = jnp.where(delta == 0, 3.0, jnp.where(delta == d4, 2.0, 1.0))
        blocks = [(pl.ds(a_buf, t_new), jnp.where(delta >= 0, 0.0, NEG), mult),
                  (pl.ds(a_buf - BAND, BAND), jnp.where(rowb >= t, 0.0, NEG), None),
                  (pl.ds(a_buf - BAND * d4 + t % d4, BAND, stride=d4), jnp.where(rowb >= t // d4, 0.0, NEG), None),
                  (pl.ds(a_buf - BAND * d16 + t, BAND, stride=d16), None, None)]
        scores = []
        m = None
        for rows, msk, _ in blocks:
            s = seg_scores(rows, q_cols)
            if msk is not None:
                s = s + msk
            scores.append(s)
            bm = jnp.max(s, axis=0, keepdims=True)
            m = bm if m is None else jnp.maximum(m, bm)
        l = jnp.zeros((1, LANES), F32)
        o = [jnp.zeros((1, LANES), F32) for _ in range(ncol)]
        for (rows, _, mlt), s in zip(blocks, scores):
            p = jnp.exp(s - m)
            if mlt is not None:
                p = p * mlt
            l = l + jnp.sum(p, axis=0, keepdims=True)
            for j in range(ncol):
                o[j] = o[j] + jnp.sum(spread(p, j) * vbuf[slot, j, rows, :], axis=0, keepdims=True)
        l8 = jnp.broadcast_to(l, (SUBLANES, LANES))
        for j in range(ncol):
            o_ref[t:t + 1, j * LANES:(j + 1) * LANES] = (o[j] / spread(l8, j)[0:1]).astype(o_ref.dtype)

    @pl.when(b == nb - 1)
    def _():
        for cp in out_copies(b, slot):
            cp.wait()


def _attn_sample(q, k_new, v_new, cache_k, cache_v):
    db, t_new, _ = q.shape
    a_buf = cache_k.shape[1]
    assert t_new == SUBLANES and a_buf == W_MAX
    row = pl.BlockSpec((None, t_new, A_WIDTH), lambda bi: (bi, 0, 0))
    hbm = pl.BlockSpec(memory_space=pl.ANY)
    body = functools.partial(_attn_sample_body, a_buf=a_buf, t_new=t_new)
    return pl.pallas_call(
        body,
        grid=(db,),
        in_specs=[row, row, row, hbm, hbm],
        out_specs=[row, hbm, hbm],
        out_shape=[jax.ShapeDtypeStruct((db, t_new, A_WIDTH), F32),
                   jax.ShapeDtypeStruct(cache_k.shape, F32), jax.ShapeDtypeStruct(cache_v.shape, F32)],
        scratch_shapes=[pltpu.VMEM((2, A_WIDTH // LANES, a_buf + t_new, LANES), F32),
                        pltpu.VMEM((2, A_WIDTH // LANES, a_buf + t_new, LANES), F32),
                        pltpu.SemaphoreType.DMA((2, 2)), pltpu.SemaphoreType.DMA((2, 2))],
        compiler_params=pltpu.CompilerParams(vmem_limit_bytes=VMEM_LIMIT,
                                             dimension_semantics=("arbitrary",)),
        name="attn_sample",
    )(q, k_new, v_new, cache_k, cache_v)


def _gdn_body(xc_ref, z_ref, ab_ref, cw_ref, cs_ref, s0_ref, al_ref, dtb_ref, nw_ref,
              o_ref, nc_ref, ns_ref,
              xbuf, state, qs, ks, vs, bs, gs, *, tc, t_valid):
    g = pl.program_id(1)
    c = GDN_CHUNK
    hd = B_DK
    nh = B_HEADS
    width = nh * hd

    @pl.when(g == 0)
    def _():
        xbuf[0:SUBLANES, :] = cs_ref[...]
        state[...] = s0_ref[...]

    xbuf[SUBLANES:SUBLANES + tc, :] = xc_ref[...]
    y = xbuf[pl.ds(SUBLANES - (CONV_W - 1), tc), :] * cw_ref[0:1, :]
    for j in range(1, CONV_W):
        y = y + xbuf[pl.ds(SUBLANES - (CONV_W - 1) + j, tc), :] * cw_ref[j:j + 1, :]
    y = y * _sigmoid(y)
    tail = xbuf[t_valid:t_valid + SUBLANES, :]
    xbuf[0:SUBLANES, :] = tail
    nc_ref[...] = tail

    grp_r = lax.broadcasted_iota(jnp.int32, (width, width), 0) // hd
    grp_c = lax.broadcasted_iota(jnp.int32, (width, width), 1) // hd
    head_sum = jnp.where(grp_r == grp_c, 1.0, 0.0).astype(BF16)
    src = lax.broadcasted_iota(jnp.int32, (LANES, width), 0)
    dst = lax.broadcasted_iota(jnp.int32, (LANES, width), 1) // hd
    pick_a = jnp.where(src == dst, 1.0, 0.0).astype(BF16)
    pick_b = jnp.where(src == dst + nh, 1.0, 0.0).astype(BF16)

    q = y[:, 0:width]
    k = y[:, width:2 * width]
    qs[...] = q * lax.rsqrt(_dot_sel_rhs(q * q, head_sum) + NORM_EPS) * (hd ** -0.5)
    ks[...] = k * lax.rsqrt(_dot_sel_rhs(k * k, head_sum) + NORM_EPS)
    vs[...] = y[:, 2 * width:3 * width]
    ab = ab_ref[...]
    beta = _sigmoid(_dot_sel_rhs(ab, pick_b))
    gate = -jnp.exp(al_ref[...]) * _softplus(_dot_sel_rhs(ab, pick_a) + dtb_ref[...])
    if t_valid < tc:
        live = lax.broadcasted_iota(jnp.int32, (tc, width), 0) < t_valid
        beta = jnp.where(live, beta, 0.0)
        gate = jnp.where(live, gate, 0.0)
    bs[...] = beta
    gs[...] = gate

    ri = lax.broadcasted_iota(jnp.int32, (c, c), 0)
    ci = lax.broadcasted_iota(jnp.int32, (c, c), 1)
    tri_incl = ri >= ci
    tri_strict = ri > ci
    tri_incl_b = jnp.where(tri_incl, 1.0, 0.0).astype(BF16)
    eye = jnp.where(ri == ci, 1.0, 0.0)
    ones_b = jnp.ones((c, c), BF16)

    def chunk(ci_, carry):
        rows = pl.ds(pl.multiple_of(ci_ * c, c), c)
        qc, kc, vc, bc = qs[rows, :], ks[rows, :], vs[rows, :], bs[rows, :]
        gcum = _dot_sel_lhs(tri_incl_b, gs[rows, :])
        glast = gcum[c - 1:c, :]
        eg = jnp.exp(gcum)
        kb = kc * bc
        vb = vc * bc
        wk = kb * eg
        qg = qc * eg
        kdec = kc * jnp.exp(glast - gcum)
        outs = []
        for h in range(nh):
            sl = slice(h * hd, (h + 1) * hd)
            g_col = gcum[:, sl]
            g_row = _dot_sel_lhs(ones_b, eye * g_col)
            decay = jnp.where(tri_incl, jnp.exp(g_col - g_row), 0.0)
            kh = kc[:, sl]
            lmat = jnp.where(tri_strict, _dot3(kb[:, sl], kh, _dot_nt) * decay, 0.0)
            attn = _dot3(qc[:, sl], kh, _dot_nt) * decay
            neg = -lmat
            tinv = eye + neg
            pw = _dot3(neg, neg)
            tinv = tinv + _dot3(tinv, pw)
            n_sq = int(math.log2(c)) - 2
            for _ in range(n_sq):
                pw = _dot3(pw, pw)
                tinv = tinv + _dot3(tinv, pw)
            u = _dot3(tinv, vb[:, sl])
            w = _dot3(tinv, wk[:, sl])
            s_h = state[h]
            v_new = u - _dot3(w, s_h)
            outs.append(_dot3(qg[:, sl], s_h) + _dot3(attn, v_new))
            state[h] = s_h * jnp.exp(glast[:, sl]) + _dot3(kdec[:, sl], v_new, _dot_tn)
        o = jnp.concatenate(outs, axis=1)
        ms = _dot_sel_rhs(o * o, head_sum) * (1.0 / hd)
        zc = z_ref[rows, :]
        o_ref[rows, :] = (o * lax.rsqrt(ms + NORM_EPS) * nw_ref[...] * (zc * _sigmoid(zc))).astype(o_ref.dtype)
        return carry

    lax.fori_loop(0, tc // c, chunk, 0)
    ns_ref[...] = state[...]


def _gdn(conv_in, z, ab, conv_w, conv_state, ssm0, a_log, dt_bias, norm_w, t_valid=None):
    b, t, _ = conv_in.shape
    tc = min(t, 4 * GDN_CHUNK)
    assert t % tc == 0 and tc % GDN_CHUNK == 0
    t_valid = t if t_valid is None else t_valid
    assert t_valid == t or t == tc
    rep = lambda a: jnp.repeat(a.astype(F32), B_DK).reshape(1, B_WIDTH)
    blk = lambda w: pl.BlockSpec((None, tc, w), lambda bi, gi: (bi, gi, 0))
    const2 = lambda shape: pl.BlockSpec(shape, lambda bi, gi: (0, 0))
    body = functools.partial(_gdn_body, tc=tc, t_valid=min(t_valid, tc))
    tail_spec = pl.BlockSpec((None, SUBLANES, CONV_DIM), lambda bi, gi: (bi, 0, 0))
    conv_tail = jnp.pad(conv_state, ((0, 0), (SUBLANES - (CONV_W - 1), 0), (0, 0)))
    o, new_tail, new_state = pl.pallas_call(
        body,
        grid=(b, t // tc),
        in_specs=[blk(CONV_DIM), blk(B_WIDTH), blk(LANES), const2((CONV_W, CONV_DIM)),
                  tail_spec,
                  pl.BlockSpec((None, B_HEADS, B_DK, B_DV), lambda bi, gi: (bi, 0, 0, 0)),
                  const2((1, B_WIDTH)), const2((1, B_WIDTH)), const2((1, B_WIDTH))],
        out_specs=[blk(B_WIDTH),
                   tail_spec,
                   pl.BlockSpec((None, B_HEADS, B_DK, B_DV), lambda bi, gi: (bi, 0, 0, 0))],
        out_shape=[jax.ShapeDtypeStruct((b, t, B_WIDTH), BF16),
                   jax.ShapeDtypeStruct((b, SUBLANES, CONV_DIM), F32),
                   jax.ShapeDtypeStruct((b, B_HEADS, B_DK, B_DV), F32)],
        scratch_shapes=[pltpu.VMEM((tc + SUBLANES, CONV_DIM), F32), pltpu.VMEM((B_HEADS, B_DK, B_DV), F32)]
        + [pltpu.VMEM((tc, B_WIDTH), F32)] * 5,
        compiler_params=pltpu.CompilerParams(vmem_limit_bytes=VMEM_LIMIT,
                                             dimension_semantics=("arbitrary", "arbitrary")),
        name="gdn",
    )(conv_in, z, ab, conv_w, conv_tail, ssm0, rep(a_log), rep(dt_bias),
      jnp.tile(norm_w.astype(F32), B_HEADS).reshape(1, B_WIDTH))
    return o, new_tail[:, SUBLANES - (CONV_W - 1):], new_state


def _outproj_body(oa_ref, ob_ref, x_ref, g1_ref, sc2_ref, sh2_ref, w_ref, lg_ref, lb_ref, rw_ref, rb_ref,
                  x1_ref, h2_ref, ti_ref, tw_ref, *, dn_alpha):
    mix = _dot(oa_ref[...], w_ref[0:A_WIDTH, :]) + _dot(ob_ref[...], w_ref[A_WIDTH:A_WIDTH + B_WIDTH, :])
    x1 = _layer_norm(dn_alpha * x_ref[...] + (1.0 + g1_ref[...]) * mix, lg_ref[...], lb_ref[...])
    x1_ref[...] = x1
    h2 = x1 * (1.0 + sc2_ref[...]) + sh2_ref[...]
    h2_ref[...] = h2.astype(BF16)
    logits = _dot3(h2, rw_ref[...]) + rb_ref[...]
    lane = lax.broadcasted_iota(jnp.int32, logits.shape, 1)
    ti = jnp.zeros(logits.shape, jnp.int32)
    tv = jnp.zeros(logits.shape, F32)
    vals = []
    for kk in range(TOP_K):
        m = jnp.max(logits, axis=-1, keepdims=True)
        idx = jnp.min(jnp.where(logits == m, lane.astype(F32), float(LANES)), axis=-1, keepdims=True).astype(jnp.int32)
        vals.append(m)
        ti = jnp.where(lane == kk, idx, ti)
        logits = jnp.where(lane == idx, NEG, logits)
    es = [jnp.exp(v - vals[0]) for v in vals]
    den = es[0] + es[1] + es[2] + es[3]
    for kk in range(TOP_K):
        tv = jnp.where(lane == kk, es[kk] / den, tv)
    ti_ref[...] = ti
    tw_ref[...] = tv


def _out_proj(oa, ob, x, g1, sc2, sh2, w_out_b, ln_g, ln_b, rw_pad, rb_pad, tm, dn_alpha):
    b, t, _ = x.shape
    per_row = g1.shape[1] != 1
    mod_spec = (pl.BlockSpec((None, tm, D_MODEL), lambda bi, i: (bi, i, 0)) if per_row
                else pl.BlockSpec((None, 1, D_MODEL), lambda bi, i: (bi, 0, 0)))
    blk = lambda w: pl.BlockSpec((None, tm, w), lambda bi, i: (bi, i, 0))
    const2 = lambda shape: pl.BlockSpec(shape, lambda bi, i: (0, 0))
    return pl.pallas_call(
        functools.partial(_outproj_body, dn_alpha=dn_alpha),
        grid=(b, t // tm),
        in_specs=[blk(A_WIDTH), blk(B_WIDTH), blk(D_MODEL), mod_spec, mod_spec, mod_spec,
                  const2((D_MODEL, D_MODEL)), const2((1, D_MODEL)), const2((1, D_MODEL)),
                  const2((D_MODEL, LANES)), const2((1, LANES))],
        out_specs=[blk(D_MODEL), blk(D_MODEL), blk(LANES), blk(LANES)],
        out_shape=[jax.ShapeDtypeStruct((b, t, D_MODEL), F32), jax.ShapeDtypeStruct((b, t, D_MODEL), BF16),
                   jax.ShapeDtypeStruct((b, t, LANES), jnp.int32), jax.ShapeDtypeStruct((b, t, LANES), F32)],
        compiler_params=pltpu.CompilerParams(vmem_limit_bytes=VMEM_LIMIT),
        name="out_proj",
    )(oa, ob, x, g1, sc2, sh2, w_out_b, ln_g, ln_b, rw_pad, rb_pad)


def _moe_body(be_ref, nv_ref, x_ref, wgu_ref, bgu_ref, wd_ref, bd_ref, rw_ref, o_ref):
    i = pl.program_id(0)

    @pl.when(i < nv_ref[0])
    def _():
        hcat = _dot(x_ref[...], wgu_ref[...]) + bgu_ref[...]
        gate = jnp.minimum(hcat[:, :D_FF], SWIGLU_LIMIT)
        up = jnp.clip(hcat[:, D_FF:], -SWIGLU_LIMIT, SWIGLU_LIMIT)
        glu = gate * _sigmoid(SWIGLU_ALPHA * gate)
        act = (glu * (up + 1.0)).astype(BF16)
        o_ref[...] = (_dot(act, wd_ref[...]) + bd_ref[...]) * rw_ref[...]

    @pl.when(i >= nv_ref[0])
    def _():
        o_ref[...] = jnp.zeros(o_ref.shape, o_ref.dtype)


def _moe_experts(xs, blk_e, n_valid, row_w, w_gu_b, b_gu, w_down_b, b_down):
    cap = xs.shape[0]
    tm = MOE_TM
    grid_spec = pltpu.PrefetchScalarGridSpec(
        num_scalar_prefetch=2,
        grid=(cap // tm,),
        in_specs=[pl.BlockSpec((tm, D_MODEL), lambda i, be, nv: (i, 0)),
                  pl.BlockSpec((None, D_MODEL, 2 * D_FF), lambda i, be, nv: (be[i], 0, 0)),
                  pl.BlockSpec((None, 1, 2 * D_FF), lambda i, be, nv: (be[i], 0, 0)),
                  pl.BlockSpec((None, D_FF, D_MODEL), lambda i, be, nv: (be[i], 0, 0)),
                  pl.BlockSpec((None, 1, D_MODEL), lambda i, be, nv: (be[i], 0, 0)),
                  pl.BlockSpec((tm, 1), lambda i, be, nv: (i, 0))],
        out_specs=pl.BlockSpec((tm, D_MODEL), lambda i, be, nv: (i, 0)),
    )
    return pl.pallas_call(
        _moe_body,
        grid_spec=grid_spec,
        out_shape=jax.ShapeDtypeStruct((cap, D_MODEL), F32),
        compiler_params=pltpu.CompilerParams(vmem_limit_bytes=VMEM_LIMIT),
        name="moe_experts",
    )(blk_e, n_valid, xs, w_gu_b, b_gu.reshape(N_EXPERTS, 1, 2 * D_FF), w_down_b,
      b_down.reshape(N_EXPERTS, 1, D_MODEL), row_w.reshape(cap, 1))


def _route(top_i, top_w, n_tok):
    tm = MOE_TM
    n_assign = n_tok * TOP_K
    eid = top_i.reshape(-1)
    wts = top_w.reshape(-1)
    order = jnp.argsort(eid, stable=True).astype(jnp.int32)
    e_s = eid[order]
    counts = jnp.sum((eid[:, None] == jnp.arange(N_EXPERTS, dtype=jnp.int32)[None, :]).astype(jnp.int32), axis=0)
    start = jnp.cumsum(counts) - counts
    padded = (counts + tm - 1) // tm * tm
    pstart = jnp.cumsum(padded) - padded
    pend = pstart + padded
    n_blocks = -(-(n_assign + N_EXPERTS * (tm - 1)) // tm)
    cap = n_blocks * tm
    blk_e = jnp.minimum(jnp.searchsorted(pend, jnp.arange(n_blocks, dtype=jnp.int32) * tm, side='right'),
                        N_EXPERTS - 1).astype(jnp.int32)
    slot_e = jnp.repeat(blk_e, tm)
    rank = jnp.arange(cap, dtype=jnp.int32) - pstart[slot_e]
    live = (rank < counts[slot_e]) & (jnp.arange(cap, dtype=jnp.int32) < pend[N_EXPERTS - 1])
    src = jnp.clip(start[slot_e] + rank, 0, n_assign - 1)
    assign = order[src]
    row_tok = jnp.where(live, assign // TOP_K, 0)
    row_w = jnp.where(live, wts[assign], 0.0)
    dest_sorted = pstart[e_s] + jnp.arange(n_assign, dtype=jnp.int32) - start[e_s]
    inv = jnp.argsort(order).astype(jnp.int32)
    dest = dest_sorted[inv]
    n_valid = (pend[N_EXPERTS - 1] // tm).astype(jnp.int32).reshape(1)
    return row_tok, row_w, blk_e, n_valid, dest


def _final_body(x1_ref, moe_ref, g2_ref, lg_ref, lb_ref, o_ref, *, dn_alpha):
    y = dn_alpha * x1_ref[...] + (1.0 + g2_ref[...]) * moe_ref[...]
    o_ref[...] = _layer_norm(y, lg_ref[...], lb_ref[...])


def _final_norm(x1, moe_out, g2, ln_g, ln_b, tm, dn_alpha):
    b, t, _ = x1.shape
    per_row = g2.shape[1] != 1
    mod_spec = (pl.BlockSpec((None, tm, D_MODEL), lambda bi, i: (bi, i, 0)) if per_row
                else pl.BlockSpec((None, 1, D_MODEL), lambda bi, i: (bi, 0, 0)))
    blk = pl.BlockSpec((None, tm, D_MODEL), lambda bi, i: (bi, i, 0))
    const2 = pl.BlockSpec((1, D_MODEL), lambda bi, i: (0, 0))
    return pl.pallas_call(
        functools.partial(_final_body, dn_alpha=dn_alpha),
        grid=(b, t // tm),
        in_specs=[blk, blk, mod_spec, const2, const2],
        out_specs=blk,
        out_shape=jax.ShapeDtypeStruct((b, t, D_MODEL), F32),
        name="final_norm",
    )(x1, moe_out, g2, ln_g, ln_b)


def _layer(x_prompt, x_sample, c_prompt, c_sample, cache_k, cache_v, conv_state, ssm_state, past_len, depth,
           w_ada, b_ada, w_in, conv_w, a_log, dt_bias, gdn_norm_w, w_out, ln1_g, ln1_b,
           router_w, router_b, w_gu, b_gu, w_down, b_down, ln2_g, ln2_b):
    bp, tp, _ = x_prompt.shape
    db, ts, _ = x_sample.shape
    dn_alpha = (2.0 * depth) ** 0.25
    n_s = db * ts
    tm = 512
    row2 = lambda a: a.reshape(1, -1)

    mod = _ada_mod(jnp.concatenate([c_prompt, c_sample], axis=0), w_ada, b_ada)
    mod_p = [m[:, None, :] for m in jnp.split(mod[:bp], 6, axis=-1)]
    mod_s = [jnp.repeat(m, ts, axis=0)[None] for m in jnp.split(mod[bp:], 6, axis=-1)]

    w_in_b = jnp.pad(w_in, ((0, 0), (0, N_IN_PAD - N_IN))).astype(BF16)
    tab_p = _rope_tables(jnp.arange(tp, dtype=jnp.int32))
    tab_s = _rope_tables(jnp.tile(past_len + jnp.arange(ts, dtype=jnp.int32), db))
    xs_flat = x_sample.reshape(1, n_s, D_MODEL)

    qp, kp, vp, cin_p, z_p, ab_p = _in_proj(x_prompt, mod_p[1], mod_p[0], tab_p, w_in_b, tm)
    qs, ks, vs, cin_s, z_s, ab_s = _in_proj(xs_flat, mod_s[1], mod_s[0], tab_s, w_in_b, tm)

    oa_p = _attn_prompt(qp, kp, vp)
    keep = min(W_MAX, tp)
    new_k_p = kp[:, tp - keep:].reshape(bp, keep, A_HEADS, HEAD_DIM)
    new_v_p = vp[:, tp - keep:].reshape(bp, keep, A_HEADS, HEAD_DIM)
    a_buf = cache_k.shape[1]
    oa_s, new_k_s, new_v_s = _attn_sample(
        qs.reshape(db, ts, A_WIDTH), ks.reshape(db, ts, A_WIDTH), vs.reshape(db, ts, A_WIDTH),
        cache_k.reshape(db, a_buf, A_WIDTH), cache_v.reshape(db, a_buf, A_WIDTH))
    new_k_s = new_k_s.reshape(db, a_buf, A_HEADS, HEAD_DIM)
    new_v_s = new_v_s.reshape(db, a_buf, A_HEADS, HEAD_DIM)

    ob_p, new_conv_p, new_ssm_p = _gdn(
        cin_p, z_p, ab_p, conv_w, jnp.zeros((bp, CONV_W - 1, CONV_DIM), F32),
        jnp.zeros((bp, B_HEADS, B_DK, B_DV), F32), a_log, dt_bias, gdn_norm_w)
    pad_t = lambda a: jnp.pad(a.reshape(db, ts, a.shape[-1]), ((0, 0), (0, GDN_CHUNK - ts), (0, 0)))
    ob_s, new_conv_s, new_ssm_s = _gdn(pad_t(cin_s), pad_t(z_s), pad_t(ab_s), conv_w, conv_state, ssm_state,
                                       a_log, dt_bias, gdn_norm_w, t_valid=ts)
    ob_s = ob_s[:, :ts]

    w_out_b = w_out.astype(BF16)
    rw_pad = jnp.pad(router_w, ((0, 0), (0, LANES - N_EXPERTS)))
    rb_pad = jnp.pad(router_b, (0, LANES - N_EXPERTS), constant_values=NEG).reshape(1, LANES)
    x1_p, h2_p, ti_p, tw_p = _out_proj(oa_p, ob_p, x_prompt, mod_p[2], mod_p[4], mod_p[3], w_out_b,
                                       row2(ln1_g), row2(ln1_b), rw_pad, rb_pad, tm, dn_alpha)
    x1_s, h2_s, ti_s, tw_s = _out_proj(oa_s.astype(BF16).reshape(1, n_s, A_WIDTH), ob_s.reshape(1, n_s, B_WIDTH), xs_flat,
                                       mod_s[2], mod_s[4], mod_s[3], w_out_b,
                                       row2(ln1_g), row2(ln1_b), rw_pad, rb_pad, tm, dn_alpha)

    n_p = bp * tp
    n_tok = n_p + n_s
    h2 = jnp.concatenate([h2_p.reshape(n_p, D_MODEL), h2_s.reshape(n_s, D_MODEL)], axis=0)
    top_i = jnp.concatenate([ti_p.reshape(n_p, LANES), ti_s.reshape(n_s, LANES)], axis=0)[:, :TOP_K]
    top_w = jnp.concatenate([tw_p.reshape(n_p, LANES), tw_s.reshape(n_s, LANES)], axis=0)[:, :TOP_K]
    row_tok, row_w, blk_e, n_valid, dest = _route(top_i, top_w, n_tok)
    xs_sorted = jnp.take(h2, row_tok, axis=0)
    ys = _moe_experts(xs_sorted, blk_e, n_valid, row_w, w_gu.astype(BF16), b_gu, w_down.astype(BF16), b_down)
    moe_out = jnp.take(ys, dest, axis=0).reshape(n_tok, TOP_K, D_MODEL).sum(axis=1)

    y_p = _final_norm(x1_p, moe_out[:n_p].reshape(bp, tp, D_MODEL), mod_p[5], row2(ln2_g), row2(ln2_b), tm, dn_alpha)
    y_s = _final_norm(x1_s, moe_out[n_p:].reshape(1, n_s, D_MODEL), mod_s[5], row2(ln2_g), row2(ln2_b), tm, dn_alpha)
    return (y_p, y_s.reshape(db, ts, D_MODEL), new_k_p, new_v_p, new_conv_p, new_ssm_p,
            new_k_s, new_v_s, new_conv_s, new_ssm_s)


def kernel(x_prompt, x_sample, c_prompt, c_sample, cache_a_k, cache_a_v, state_b_conv, state_b_ssm, w_ada, b_ada, w_in, conv_w, a_log, dt_bias, gdn_norm_w, w_out, ln1_g, ln1_b, router_w, router_b, w_gu, b_gu, w_down, b_down, ln2_g, ln2_b):
    depth = w_ada.shape[0]
    assert depth == 1
    past_len = 8192
    outs = _layer(x_prompt, x_sample, c_prompt, c_sample, cache_a_k[0], cache_a_v[0], state_b_conv[0],
                  state_b_ssm[0], past_len, depth,
                  w_ada[0], b_ada[0], w_in[0], conv_w[0], a_log[0], dt_bias[0], gdn_norm_w[0], w_out[0],
                  ln1_g[0], ln1_b[0], router_w[0], router_b[0], w_gu[0], b_gu[0], w_down[0], b_down[0],
                  ln2_g[0], ln2_b[0])
    y_p, y_s = outs[0], outs[1]
    return (y_p, y_s) + tuple(o[None] for o in outs[2:])
```

```python
import functools
import math

import jax
import jax.numpy as jnp
from jax import lax
from jax.experimental import pallas as pl
from jax.experimental.pallas import tpu as pltpu

F32 = jnp.float32
BF16 = jnp.bfloat16

D_MODEL = 1024
HEAD_DIM = 64
A_HEADS = 12
B_HEADS = 4
B_DK = 64
B_DV = 64
A_WIDTH = A_HEADS * HEAD_DIM
B_WIDTH = B_HEADS * B_DV
DILATED_BRANCHES = ((128, 1), (512, 4), (2048, 16))
W_MAX = 2048
BAND = 128
ATTN_SCALE = HEAD_DIM ** -0.5
ROPE_DIM = HEAD_DIM // 4
ROPE_THETA = 500000.0
CONV_W = 4
CONV_DIM = 2 * B_HEADS * B_DK + B_HEADS * B_DV
N_EXPERTS = 32
TOP_K = 4
D_FF = D_MODEL
SWIGLU_LIMIT = 7.0
SWIGLU_ALPHA = 1.702
LN_EPS = 1e-5
NORM_EPS = 1e-6
N_IN = 3 * A_WIDTH + CONV_DIM + B_WIDTH + 2 * B_HEADS

LANES = 128
SUBLANES = 8
N_IN_PAD = -(-N_IN // LANES) * LANES
GDN_CHUNK = 64
GDN_CHUNK_SAMPLE = 16
MOE_TM = 512
NEG = -1e30
VMEM_LIMIT = 56 * 1024 * 1024


def _dot(a, b):
    return jnp.dot(a, b, preferred_element_type=F32)


def _dot_nt(a, b):
    return lax.dot_general(a, b, (((1,), (1,)), ((), ())), preferred_element_type=F32)


def _dot_tn(a, b):
    return lax.dot_general(a, b, (((0,), (0,)), ((), ())), preferred_element_type=F32)


def _split2(x):
    hi = x.astype(BF16)
    lo = (x - hi.astype(F32)).astype(BF16)
    return hi, lo


def _split3(x):
    hi = x.astype(BF16)
    r = x - hi.astype(F32)
    mid = r.astype(BF16)
    lo = (r - mid.astype(F32)).astype(BF16)
    return hi, mid, lo


def _dot3(a, b, dot=_dot):
    ah, al = _split2(a)
    bh, bl = _split2(b)
    return dot(ah, bh) + (dot(ah, bl) + dot(al, bh))


def _dot_sel_rhs(a, sel_bf16):
    hi, lo = _split2(a)
    return _dot(hi, sel_bf16) + _dot(lo, sel_bf16)


def _dot_sel_lhs(sel_bf16, b):
    hi, lo = _split2(b)
    return _dot(sel_bf16, hi) + _dot(sel_bf16, lo)


def _sigmoid(x):
    return 1.0 / (1.0 + jnp.exp(-x))


def _softplus(x):
    return jnp.maximum(x, 0.0) + jnp.log(1.0 + jnp.exp(-jnp.abs(x)))


def _layer_norm(y, g, b):
    mu = jnp.mean(y, axis=-1, keepdims=True)
    yc = y - mu
    var = jnp.mean(yc * yc, axis=-1, keepdims=True)
    return yc * lax.rsqrt(var + LN_EPS) * g + b


def _ada_body(c_ref, w_ref, b_ref, o_ref):
    c = c_ref[...]
    s = c * _sigmoid(c)
    o_ref[...] = _dot3(s, w_ref[...]) + b_ref[...]


def _ada_mod(c, w_ada, b_ada):
    n = c.shape[0]
    n_pad = -(-n // SUBLANES) * SUBLANES
    c = jnp.pad(c, ((0, n_pad - n), (0, 0)))
    tn = 512
    out = pl.pallas_call(
        _ada_body,
        grid=(w_ada.shape[1] // tn,),
        in_specs=[pl.BlockSpec((n_pad, D_MODEL), lambda j: (0, 0)),
                  pl.BlockSpec((D_MODEL, tn), lambda j: (0, j)),
                  pl.BlockSpec((1, tn), lambda j: (0, j))],
        out_specs=pl.BlockSpec((n_pad, tn), lambda j: (0, j)),
        out_shape=jax.ShapeDtypeStruct((n_pad, w_ada.shape[1]), F32),
        name="ada_mod",
    )(c, w_ada, b_ada.reshape(1, -1))
    return out[:n]


def _rope_tables(pos):
    half = ROPE_DIM // 2
    inv = ROPE_THETA ** (-jnp.arange(half, dtype=F32) / half)
    ang = pos.astype(F32)[:, None] * inv[None, :]
    cos, sin = jnp.cos(ang), jnp.sin(ang)
    t = pos.shape[0]
    pad = jnp.zeros((t, HEAD_DIM - ROPE_DIM), F32)
    zero = jnp.zeros((t, half), F32)
    ct = jnp.concatenate([cos, cos, pad + 1.0], axis=1)
    s1 = jnp.concatenate([-sin, zero, pad], axis=1)
    s2 = jnp.concatenate([zero, sin, pad], axis=1)
    rep = LANES // HEAD_DIM
    return tuple(jnp.tile(a, (1, rep)) for a in (ct, s1, s2))


def _inproj_body(x_ref, sc_ref, sh_ref, ct_ref, s1_ref, s2_ref, w_ref,
                 q_ref, k_ref, v_ref, c_ref, z_ref, ab_ref):
    h = x_ref[...] * (1.0 + sc_ref[...]) + sh_ref[...]
    hb = h.astype(BF16)
    ct, s1, s2 = ct_ref[...], s1_ref[...], s2_ref[...]

    def rope(t):
        return t * ct + pltpu.roll(t, LANES - ROPE_DIM // 2, 1) * s1 + pltpu.roll(t, ROPE_DIM // 2, 1) * s2

    for j in range(A_WIDTH // LANES):
        cols = slice(LANES * j, LANES * (j + 1))
        q_ref[:, cols] = rope(_dot(hb, w_ref[:, LANES * j:LANES * (j + 1)])) * ATTN_SCALE
        k_ref[:, cols] = rope(_dot(hb, w_ref[:, A_WIDTH + LANES * j:A_WIDTH + LANES * (j + 1)]))
    o1 = 3 * A_WIDTH
    o2 = o1 + CONV_DIM
    o3 = o2 + B_WIDTH
    v_ref[...] = _dot(hb, w_ref[:, 2 * A_WIDTH:o1])
    c_ref[...] = _dot(hb, w_ref[:, o1:o2])
    z_ref[...] = _dot(hb, w_ref[:, o2:o3])
    ab_ref[...] = _dot(hb, w_ref[:, o3:N_IN_PAD])


def _in_proj(x, sc, sh, tables, w_pad, tm):
    b, t, _ = x.shape
    per_row = sc.shape[1] != 1
    mod_spec = (pl.BlockSpec((None, tm, D_MODEL), lambda bi, i: (bi, i, 0)) if per_row
                else pl.BlockSpec((None, 1, D_MODEL), lambda bi, i: (bi, 0, 0)))
    tab_spec = pl.BlockSpec((tm, LANES), lambda bi, i: (i, 0))

    def out(width):
        return (pl.BlockSpec((None, tm, width), lambda bi, i: (bi, i, 0)),
                jax.ShapeDtypeStruct((b, t, width), F32))

    outs = [out(A_WIDTH), out(A_WIDTH), out(A_WIDTH), out(CONV_DIM), out(B_WIDTH), out(LANES)]
    return pl.pallas_call(
        _inproj_body,
        grid=(b, t // tm),
        in_specs=[pl.BlockSpec((None, tm, D_MODEL), lambda bi, i: (bi, i, 0)), mod_spec, mod_spec,
                  tab_spec, tab_spec, tab_spec,
                  pl.BlockSpec((D_MODEL, N_IN_PAD), lambda bi, i: (0, 0))],
        out_specs=[o[0] for o in outs],
        out_shape=[o[1] for o in outs],
        compiler_params=pltpu.CompilerParams(vmem_limit_bytes=VMEM_LIMIT),
        name="in_proj",
    )(x, sc, sh, *tables, w_pad)


ATTN_TQ = 2048


def _attn_body(q_ref, kc_ref, kp_ref, vc_ref, vp_ref, o_ref, kbuf, vbuf, acc, mst, lst, bias):
    tq = ATTN_TQ
    first_tile = pl.program_id(1) == 0
    kbuf[0:tq, :] = kp_ref[...]
    kbuf[tq:2 * tq, :] = kc_ref[...]
    vbuf[0:tq, :] = vp_ref[...]
    vbuf[tq:2 * tq, :] = vc_ref[...]

    qi = lax.broadcasted_iota(jnp.int32, (BAND, 2 * BAND), 0)
    ki = lax.broadcasted_iota(jnp.int32, (BAND, 2 * BAND), 1)
    dist = qi + BAND - ki
    bias[0] = jnp.where((dist >= 0) & (dist <= BAND), 0.0, NEG)
    bias[1] = jnp.where(ki >= BAND, 0.0, NEG)
    lo = lax.broadcasted_iota(jnp.int32, (BAND, LANES), 1) < HEAD_DIM

    for r, (window, d) in enumerate(DILATED_BRANCHES):
        assert window // d == BAND
        span = BAND * d

        def blk(idx, carry, r=r, d=d, span=span):
            c = idx % d
            j = idx // d
            qs = j * span + c
            ks = tq + qs - span
            if d == 1:
                qs = pl.multiple_of(qs, BAND)
                ks = pl.multiple_of(ks, BAND)
                q_idx, k_idx = pl.ds(qs, BAND), pl.ds(ks, 2 * BAND)
            else:
                q_idx, k_idx = pl.ds(qs, BAND, stride=d), pl.ds(ks, 2 * BAND, stride=d)
            qb = q_ref[q_idx, :]
            kb = kbuf[k_idx, :].astype(BF16)
            vb = vbuf[k_idx, :].astype(BF16)
            no_prev = jnp.where(first_tile & (j == 0), 1.0, 0.0)
            mask = bias[0] + bias[1] * no_prev
            res = []
            for hsel in (lo, jnp.logical_not(lo)):
                qh = jnp.where(hsel, qb, 0.0).astype(BF16)
                s = _dot_nt(qh, kb) + mask
                m = jnp.max(s, axis=-1, keepdims=True)
                p = jnp.exp(s - m)
                l = jnp.sum(p, axis=-1, keepdims=True)
                res.append((_dot(p.astype(BF16), vb), m, l))
            (o0, m0, l0), (o1, m1, l1) = res
            acc[r, q_idx, :] = jnp.where(lo, o0, o1)
            mst[r, q_idx, :] = jnp.where(lo, m0, m1)
            lst[r, q_idx, :] = jnp.where(lo, l0, l1)
            return carry

        lax.fori_loop(0, tq // BAND, blk, 0, unroll=4)

    def combine(t, carry):
        rows = pl.ds(pl.multiple_of(t * BAND, BAND), BAND)
        ms = [mst[r, rows, :] for r in range(3)]
        m = jnp.maximum(jnp.maximum(ms[0], ms[1]), ms[2])
        ws = [jnp.exp(mr - m) for mr in ms]
        num = ws[0] * acc[0, rows, :] + ws[1] * acc[1, rows, :] + ws[2] * acc[2, rows, :]
        den = ws[0] * lst[0, rows, :] + ws[1] * lst[1, rows, :] + ws[2] * lst[2, rows, :]
        o_ref[rows, :] = (num / den).astype(o_ref.dtype)
        return carry

    lax.fori_loop(0, tq // BAND, combine, 0)


def _attn_prompt(q, k, v):
    b, t, _ = q.shape
    tq = ATTN_TQ
    assert t % tq == 0
    cur = pl.BlockSpec((None, tq, LANES), lambda bi, i, hp: (bi, i, hp))
    prev = pl.BlockSpec((None, tq, LANES), lambda bi, i, hp: (bi, jnp.maximum(i - 1, 0), hp))
    return pl.pallas_call(
        _attn_body,
        grid=(b, t // tq, A_WIDTH // LANES),
        in_specs=[cur, cur, prev, cur, prev],
        out_specs=cur,
        out_shape=jax.ShapeDtypeStruct((b, t, A_WIDTH), BF16),
        scratch_shapes=[pltpu.VMEM((2 * tq, LANES), F32), pltpu.VMEM((2 * tq, LANES), F32),
                        pltpu.VMEM((3, tq, LANES), F32), pltpu.VMEM((3, tq, LANES), F32),
                        pltpu.VMEM((3, tq, LANES), F32), pltpu.VMEM((2, BAND, 2 * BAND), F32)],
        compiler_params=pltpu.CompilerParams(vmem_limit_bytes=VMEM_LIMIT),
        name="attn_prompt",
    )(q, k, k, v, v)


def _branch_count(delta):
    cnt = jnp.zeros(delta.shape, F32)
    for window, d in DILATED_BRANCHES:
        hit = (delta >= 0) & (delta <= window) & ((delta & (d - 1)) == 0)
        cnt = cnt + jnp.where(hit, 1.0, 0.0)
    return cnt


def _attn_sample_body(q_ref, kn_ref, vn_ref, ck_hbm, cv_hbm, o_ref, ok_hbm, ov_hbm,
                      inb, outb, p_cache, p_new, placed, in_sem, out_sem, *, a_buf, t_new):
    b = pl.program_id(0)
    w = pl.program_id(1)
    nb = pl.num_programs(0)
    srcs = (ck_hbm, cv_hbm)
    dsts = (ok_hbm, ov_hbm)
    n_tiles = a_buf // LANES
    new0 = LANES - t_new

    def in_copy(ws, bi):
        return pltpu.make_async_copy(srcs[ws].at[bi], inb.at[ws], in_sem.at[ws])

    def out_copy(ws, bi):
        return pltpu.make_async_copy(outb.at[ws], dsts[ws].at[bi], out_sem.at[ws])

    lane = lax.broadcasted_iota(jnp.int32, (t_new, LANES), 1)
    qrow = lax.broadcasted_iota(jnp.int32, (t_new, LANES), 0)
    place = jnp.where(lane == new0 + qrow, 1.0, 0.0).astype(BF16)

    def place_new(x_ref):
        hi, mid, lo = _split3(x_ref[...])
        xt = _dot_tn(hi, place) + (_dot_tn(mid, place) + _dot_tn(lo, place))
        return xt.reshape(A_HEADS, HEAD_DIM, LANES)

    def shift_into(ws):
        keep = lax.broadcasted_iota(jnp.int32, (HEAD_DIM, LANES), 1) < new0

        def per_head(h, carry):
            prev = pltpu.roll(inb[ws, h, :, 0:LANES], new0, 1)
            for j in range(n_tiles):
                if j + 1 < n_tiles:
                    nxt = pltpu.roll(inb[ws, h, :, (j + 1) * LANES:(j + 2) * LANES], new0, 1)
                else:
                    nxt = placed[h]
                outb[ws, h, :, j * LANES:(j + 1) * LANES] = jnp.where(keep, prev, nxt)
                prev = nxt
            return carry

        lax.fori_loop(0, A_HEADS, per_head, 0)

    @pl.when(w == 0)
    def _():
        @pl.when(b == 0)
        def _():
            in_copy(0, 0).start()

        in_copy(1, b).start()
        placed[...] = place_new(kn_ref)
        in_copy(0, b).wait()

        pos = lax.broadcasted_iota(jnp.int32, (t_new, a_buf), 1)
        cnt = _branch_count(a_buf + lax.broadcasted_iota(jnp.int32, (t_new, a_buf), 0) - pos)
        cnt_new = jnp.where(lane >= new0, _branch_count(qrow - (lane - new0)), 0.0)
        q_all = q_ref[...]
        for h in range(A_HEADS):
            qh = q_all[:, h * HEAD_DIM:(h + 1) * HEAD_DIM].astype(BF16)
            s = jnp.where(cnt > 0.0, _dot(qh, inb[0, h].astype(BF16)), NEG)
            sn = jnp.where(cnt_new > 0.0, _dot(qh, placed[h].astype(BF16)), NEG)
            m = jnp.maximum(jnp.max(s, axis=-1, keepdims=True), jnp.max(sn, axis=-1, keepdims=True))
            p = cnt * jnp.exp(s - m)
            pn = cnt_new * jnp.exp(sn - m)
            inv_l = 1.0 / (jnp.sum(p, axis=-1, keepdims=True) + jnp.sum(pn, axis=-1, keepdims=True))
            p_cache[h] = p * inv_l
            p_new[h] = pn * inv_l

        @pl.when(b > 0)
        def _():
            out_copy(0, b - 1).wait()

        shift_into(0)
        out_copy(0, b).start()

    @pl.when(w == 1)
    def _():
        @pl.when(b + 1 < nb)
        def _():
            in_copy(0, b + 1).start()

        placed[...] = place_new(vn_ref)
        in_copy(1, b).wait()
        for h in range(A_HEADS):
            o_h = (_dot_nt(p_cache[h].astype(BF16), inb[1, h].astype(BF16))
                   + _dot_nt(p_new[h].astype(BF16), placed[h].astype(BF16)))
            o_ref[:, h * HEAD_DIM:(h + 1) * HEAD_DIM] = o_h

        @pl.when(b > 0)
        def _():
            out_copy(1, b - 1).wait()

        shift_into(1)
        out_copy(1, b).start()

        @pl.when(b == nb - 1)
        def _():
            out_copy(0, b).wait()
            out_copy(1, b).wait()


def _attn_sample(q, k_new, v_new, cache_kt, cache_vt):
    db, t_new, _ = q.shape
    a_buf = cache_kt.shape[-1]
    assert t_new == SUBLANES and a_buf == W_MAX and cache_kt.shape[1:3] == (A_HEADS, HEAD_DIM)
    row = pl.BlockSpec((None, t_new, A_WIDTH), lambda bi, wi: (bi, 0, 0))
    hbm = pl.BlockSpec(memory_space=pl.ANY)
    body = functools.partial(_attn_sample_body, a_buf=a_buf, t_new=t_new)
    return pl.pallas_call(
        body,
        grid=(db, 2),
        in_specs=[row, row, row, hbm, hbm],
        out_specs=[row, hbm, hbm],
        out_shape=[jax.ShapeDtypeStruct((db, t_new, A_WIDTH), F32),
                   jax.ShapeDtypeStruct(cache_kt.shape, F32), jax.ShapeDtypeStruct(cache_vt.shape, F32)],
        scratch_shapes=[pltpu.VMEM((2, A_HEADS, HEAD_DIM, a_buf), F32), pltpu.VMEM((2, A_HEADS, HEAD_DIM, a_buf), F32),
                        pltpu.VMEM((A_HEADS, t_new, a_buf), F32), pltpu.VMEM((A_HEADS, t_new, LANES), F32),
                        pltpu.VMEM((A_HEADS, HEAD_DIM, LANES), F32),
                        pltpu.SemaphoreType.DMA((2,)), pltpu.SemaphoreType.DMA((2,))],
        compiler_params=pltpu.CompilerParams(vmem_limit_bytes=VMEM_LIMIT,
                                             dimension_semantics=("arbitrary", "arbitrary")),
        name="attn_sample",
    )(q, k_new, v_new, cache_kt, cache_vt)


def _gdn_body(xc_ref, z_ref, ab_ref, cw_ref, cs_ref, s0_ref, al_ref, dtb_ref, nw_ref,
              o_ref, nc_ref, ns_ref,
              xbuf, state, qs, ks, vs, bs, gs, *, bb, tc, c, t_valid, unroll):
    g = pl.program_id(1)
    hd = B_DK
    nh = B_HEADS
    width = nh * hd

    @pl.when(g == 0)
    def _():
        xbuf[:, 0:SUBLANES, :] = cs_ref[...]
        state[...] = s0_ref[...]

    grp_r = lax.broadcasted_iota(jnp.int32, (width, width), 0) // hd
    grp_c = lax.broadcasted_iota(jnp.int32, (width, width), 1) // hd
    head_sum = jnp.where(grp_r == grp_c, 1.0, 0.0).astype(BF16)
    src = lax.broadcasted_iota(jnp.int32, (LANES, width), 0)
    dst = lax.broadcasted_iota(jnp.int32, (LANES, width), 1) // hd
    pick_a = jnp.where(src == dst, 1.0, 0.0).astype(BF16)
    pick_b = jnp.where(src == dst + nh, 1.0, 0.0).astype(BF16)

    for bi in range(bb):
        xbuf[bi, SUBLANES:SUBLANES + tc, :] = xc_ref[bi]
        y = xbuf[bi, pl.ds(SUBLANES - (CONV_W - 1), tc), :] * cw_ref[0:1, :]
        for j in range(1, CONV_W):
            y = y + xbuf[bi, pl.ds(SUBLANES - (CONV_W - 1) + j, tc), :] * cw_ref[j:j + 1, :]
        y = y * _sigmoid(y)
        tail = xbuf[bi, t_valid:t_valid + SUBLANES, :]
        xbuf[bi, 0:SUBLANES, :] = tail
        nc_ref[bi] = tail

        q = y[:, 0:width]
        k = y[:, width:2 * width]
        qs[bi] = q * lax.rsqrt(_dot_sel_rhs(q * q, head_sum) + NORM_EPS) * (hd ** -0.5)
        ks[bi] = k * lax.rsqrt(_dot_sel_rhs(k * k, head_sum) + NORM_EPS)
        vs[bi] = y[:, 2 * width:3 * width]
        ab = ab_ref[bi]
        beta = _sigmoid(_dot_sel_rhs(ab, pick_b))
        gate = -jnp.exp(al_ref[...]) * _softplus(_dot_sel_rhs(ab, pick_a) + dtb_ref[...])
        if t_valid < tc:
            live = lax.broadcasted_iota(jnp.int32, (tc, width), 0) < t_valid
            beta = jnp.where(live, beta, 0.0)
            gate = jnp.where(live, gate, 0.0)
        bs[bi] = beta
        gs[bi] = gate

    ri = lax.broadcasted_iota(jnp.int32, (c, c), 0)
    ci = lax.broadcasted_iota(jnp.int32, (c, c), 1)
    tri_incl = ri >= ci
    tri_strict = ri > ci
    tri_incl_b = jnp.where(tri_incl, 1.0, 0.0).astype(BF16)
    eye = jnp.where(ri == ci, 1.0, 0.0)
    ones_b = jnp.ones((c, c), BF16)
    n_sq = int(math.log2(c)) - 2

    def chunk(ci_, carry):
        rows = pl.ds(pl.multiple_of(ci_ * c, c), c)
        for bi in range(bb):
            qc, kc, vc, bc = qs[bi, rows, :], ks[bi, rows, :], vs[bi, rows, :], bs[bi, rows, :]
            gcum = _dot_sel_lhs(tri_incl_b, gs[bi, rows, :])
            glast = gcum[c - 1:c, :]
            eg = jnp.exp(gcum)
            kb = kc * bc
            kcb = kc.astype(BF16)
            kbb = kb.astype(BF16)
            qcb = qc.astype(BF16)
            vbb = (vc * bc).astype(BF16)
            wkb = (kb * eg).astype(BF16)
            qgb = (qc * eg).astype(BF16)
            kdb = (kc * jnp.exp(glast - gcum)).astype(BF16)
            outs = []
            for h in range(nh):
                sl = slice(h * hd, (h + 1) * hd)
                g_col = gcum[:, h * hd:h * hd + c]
                g_row = _dot_sel_lhs(ones_b, eye * g_col)
                decay = jnp.where(tri_incl, jnp.exp(g_col - g_row), 0.0)
                lmat = jnp.where(tri_strict, _dot_nt(kbb[:, sl], kcb[:, sl]) * decay, 0.0)
                attn = (_dot_nt(qcb[:, sl], kcb[:, sl]) * decay).astype(BF16)
                neg = -lmat
                tinv = eye + neg
                negb = neg.astype(BF16)
                pw = _dot(negb, negb)
                tinv = tinv + _dot(tinv.astype(BF16), pw.astype(BF16))
                for _ in range(n_sq):
                    pwb = pw.astype(BF16)
                    pw = _dot(pwb, pwb)
                    tinv = tinv + _dot(tinv.astype(BF16), pw.astype(BF16))
                tb = tinv.astype(BF16)
                u = _dot(tb, vbb[:, sl])
                w = _dot(tb, wkb[:, sl])
                s_h = state[bi, h]
                s_hb = s_h.astype(BF16)
                v_new = (u - _dot(w.astype(BF16), s_hb)).astype(BF16)
                o_h = _dot(qgb[:, sl], s_hb) + _dot(attn, v_new)
                state[bi, h] = s_h * jnp.exp(glast[:, sl]) + _dot_tn(kdb[:, sl], v_new)
                outs.append(o_h * lax.rsqrt(jnp.mean(o_h * o_h, axis=-1, keepdims=True) + NORM_EPS))
            o = jnp.concatenate(outs, axis=1)
            zc = z_ref[bi, rows, :]
            o_ref[bi, rows, :] = (o * nw_ref[...] * (zc * _sigmoid(zc))).astype(o_ref.dtype)
        return carry

    lax.fori_loop(0, tc // c, chunk, 0, unroll=unroll)
    ns_ref[...] = state[...]


def _gdn(conv_in, z, ab, conv_w, conv_state, ssm0, a_log, dt_bias, norm_w, *, chunk, t_valid=None):
    b, t, _ = conv_in.shape
    c = chunk
    bb = 2
    tc = min(t, 4 * c)
    assert t % tc == 0 and tc % c == 0 and b % bb == 0 and c <= B_DK
    t_valid = t if t_valid is None else t_valid
    assert t_valid == t or t == tc
    n_chunks = tc // c
    rep = lambda a: jnp.repeat(a.astype(F32), B_DK).reshape(1, B_WIDTH)
    blk = lambda w: pl.BlockSpec((bb, tc, w), lambda bi, gi: (bi, gi, 0))
    const2 = lambda shape: pl.BlockSpec(shape, lambda bi, gi: (0, 0))
    body = functools.partial(_gdn_body, bb=bb, tc=tc, c=c, t_valid=min(t_valid, tc),
                             unroll=2 if n_chunks % 2 == 0 else 1)
    tail_spec = pl.BlockSpec((bb, SUBLANES, CONV_DIM), lambda bi, gi: (bi, 0, 0))
    state_spec = pl.BlockSpec((bb, B_HEADS, B_DK, B_DV), lambda bi, gi: (bi, 0, 0, 0))
    conv_tail = jnp.pad(conv_state, ((0, 0), (SUBLANES - (CONV_W - 1), 0), (0, 0)))
    o, new_tail, new_state = pl.pallas_call(
        body,
        grid=(b // bb, t // tc),
        in_specs=[blk(CONV_DIM), blk(B_WIDTH), blk(LANES), const2((CONV_W, CONV_DIM)),
                  tail_spec, state_spec,
                  const2((1, B_WIDTH)), const2((1, B_WIDTH)), const2((1, B_WIDTH))],
        out_specs=[blk(B_WIDTH), tail_spec, state_spec],
        out_shape=[jax.ShapeDtypeStruct((b, t, B_WIDTH), BF16),
                   jax.ShapeDtypeStruct((b, SUBLANES, CONV_DIM), F32),
                   jax.ShapeDtypeStruct((b, B_HEADS, B_DK, B_DV), F32)],
        scratch_shapes=[pltpu.VMEM((bb, tc + SUBLANES, CONV_DIM), F32),
                        pltpu.VMEM((bb, B_HEADS, B_DK, B_DV), F32)]
        + [pltpu.VMEM((bb, tc, B_WIDTH), F32)] * 5,
        compiler_params=pltpu.CompilerParams(vmem_limit_bytes=VMEM_LIMIT,
                                             dimension_semantics=("arbitrary", "arbitrary")),
        name="gdn",
    )(conv_in, z, ab, conv_w, conv_tail, ssm0, rep(a_log), rep(dt_bias),
      jnp.tile(norm_w.astype(F32), B_HEADS).reshape(1, B_WIDTH))
    return o, new_tail[:, SUBLANES - (CONV_W - 1):], new_state


def _outproj_body(oa_ref, ob_ref, x_ref, g1_ref, sc2_ref, sh2_ref, w_ref, lg_ref, lb_ref, rw_ref, rb_ref,
                  x1_ref, h2_ref, ti_ref, tw_ref, *, dn_alpha):
    mix = _dot(oa_ref[...], w_ref[0:A_WIDTH, :]) + _dot(ob_ref[...], w_ref[A_WIDTH:A_WIDTH + B_WIDTH, :])
    x1 = _layer_norm(dn_alpha * x_ref[...] + (1.0 + g1_ref[...]) * mix, lg_ref[...], lb_ref[...])
    x1_ref[...] = x1
    h2 = x1 * (1.0 + sc2_ref[...]) + sh2_ref[...]
    h2_ref[...] = h2.astype(BF16)
    logits = _dot3(h2, rw_ref[...]) + rb_ref[...]
    lane = lax.broadcasted_iota(jnp.int32, logits.shape, 1)
    ti = jnp.zeros(logits.shape, jnp.int32)
    tv = jnp.zeros(logits.shape, F32)
    vals = []
    for kk in range(TOP_K):
        m = jnp.max(logits, axis=-1, keepdims=True)
        idx = jnp.min(jnp.where(logits == m, lane.astype(F32), float(LANES)), axis=-1, keepdims=True).astype(jnp.int32)
        vals.append(m)
        ti = jnp.where(lane == kk, idx, ti)
        logits = jnp.where(lane == idx, NEG, logits)
    es = [jnp.exp(v - vals[0]) for v in vals]
    den = es[0] + es[1] + es[2] + es[3]
    for kk in range(TOP_K):
        tv = jnp.where(lane == kk, es[kk] / den, tv)
    ti_ref[...] = ti
    tw_ref[...] = tv


def _out_proj(oa, ob, x, g1, sc2, sh2, w_out_b, ln_g, ln_b, rw_pad, rb_pad, tm, dn_alpha):
    b, t, _ = x.shape
    per_row = g1.shape[1] != 1
    mod_spec = (pl.BlockSpec((None, tm, D_MODEL), lambda bi, i: (bi, i, 0)) if per_row
                else pl.BlockSpec((None, 1, D_MODEL), lambda bi, i: (bi, 0, 0)))
    blk = lambda w: pl.BlockSpec((None, tm, w), lambda bi, i: (bi, i, 0))
    const2 = lambda shape: pl.BlockSpec(shape, lambda bi, i: (0, 0))
    return pl.pallas_call(
        functools.partial(_outproj_body, dn_alpha=dn_alpha),
        grid=(b, t // tm),
        in_specs=[blk(A_WIDTH), blk(B_WIDTH), blk(D_MODEL), mod_spec, mod_spec, mod_spec,
                  const2((D_MODEL, D_MODEL)), const2((1, D_MODEL)), const2((1, D_MODEL)),
                  const2((D_MODEL, LANES)), const2((1, LANES))],
        out_specs=[blk(D_MODEL), blk(D_MODEL), blk(LANES), blk(LANES)],
        out_shape=[jax.ShapeDtypeStruct((b, t, D_MODEL), F32), jax.ShapeDtypeStruct((b, t, D_MODEL), BF16),
                   jax.ShapeDtypeStruct((b, t, LANES), jnp.int32), jax.ShapeDtypeStruct((b, t, LANES), F32)],
        compiler_params=pltpu.CompilerParams(vmem_limit_bytes=VMEM_LIMIT),
        name="out_proj",
    )(oa, ob, x, g1, sc2, sh2, w_out_b, ln_g, ln_b, rw_pad, rb_pad)


def _moe_body(be_ref, nv_ref, x_ref, wgu_ref, bgu_ref, wd_ref, bd_ref, rw_ref, o_ref):
    i = pl.program_id(0)

    @pl.when(i < nv_ref[0])
    def _():
        hcat = _dot(x_ref[...], wgu_ref[...]) + bgu_ref[...]
        gate = jnp.minimum(hcat[:, :D_FF], SWIGLU_LIMIT)
        up = jnp.clip(hcat[:, D_FF:], -SWIGLU_LIMIT, SWIGLU_LIMIT)
        glu = gate * _sigmoid(SWIGLU_ALPHA * gate)
        act = (glu * (up + 1.0)).astype(BF16)
        o_ref[...] = (_dot(act, wd_ref[...]) + bd_ref[...]) * rw_ref[...]

    @pl.when(i >= nv_ref[0])
    def _():
        o_ref[...] = jnp.zeros(o_ref.shape, o_ref.dtype)


def _moe_experts(xs, blk_e, n_valid, row_w, w_gu_b, b_gu, w_down_b, b_down):
    cap = xs.shape[0]
    tm = MOE_TM
    grid_spec = pltpu.PrefetchScalarGridSpec(
        num_scalar_prefetch=2,
        grid=(cap // tm,),
        in_specs=[pl.BlockSpec((tm, D_MODEL), lambda i, be, nv: (i, 0)),
                  pl.BlockSpec((None, D_MODEL, 2 * D_FF), lambda i, be, nv: (be[i], 0, 0)),
                  pl.BlockSpec((None, 1, 2 * D_FF), lambda i, be, nv: (be[i], 0, 0)),
                  pl.BlockSpec((None, D_FF, D_MODEL), lambda i, be, nv: (be[i], 0, 0)),
                  pl.BlockSpec((None, 1, D_MODEL), lambda i, be, nv: (be[i], 0, 0)),
                  pl.BlockSpec((tm, 1), lambda i, be, nv: (i, 0))],
        out_specs=pl.BlockSpec((tm, D_MODEL), lambda i, be, nv: (i, 0)),
    )
    return pl.pallas_call(
        _moe_body,
        grid_spec=grid_spec,
        out_shape=jax.ShapeDtypeStruct((cap, D_MODEL), F32),
        compiler_params=pltpu.CompilerParams(vmem_limit_bytes=VMEM_LIMIT),
        name="moe_experts",
    )(blk_e, n_valid, xs, w_gu_b, b_gu.reshape(N_EXPERTS, 1, 2 * D_FF), w_down_b,
      b_down.reshape(N_EXPERTS, 1, D_MODEL), row_w.reshape(cap, 1))


def _route(top_i, top_w, n_tok):
    tm = MOE_TM
    n_assign = n_tok * TOP_K
    eid = top_i.reshape(-1)
    wts = top_w.reshape(-1)
    order = jnp.argsort(eid, stable=True).astype(jnp.int32)
    e_s = eid[order]
    experts = jnp.arange(N_EXPERTS, dtype=jnp.int32)
    start = jnp.searchsorted(e_s, experts, side='left').astype(jnp.int32)
    counts = jnp.searchsorted(e_s, experts, side='right').astype(jnp.int32) - start
    padded = (counts + tm - 1) // tm * tm
    pstart = jnp.cumsum(padded) - padded
    pend = pstart + padded
    n_blocks = -(-(n_assign + N_EXPERTS * (tm - 1)) // tm)
    cap = n_blocks * tm
    blk_e = jnp.minimum(jnp.searchsorted(pend, jnp.arange(n_blocks, dtype=jnp.int32) * tm, side='right'),
                        N_EXPERTS - 1).astype(jnp.int32)
    slot_e = jnp.repeat(blk_e, tm)
    rank = jnp.arange(cap, dtype=jnp.int32) - pstart[slot_e]
    live = (rank < counts[slot_e]) & (jnp.arange(cap, dtype=jnp.int32) < pend[N_EXPERTS - 1])
    src = jnp.clip(start[slot_e] + rank, 0, n_assign - 1)
    assign = order[src]
    row_tok = jnp.where(live, assign // TOP_K, 0)
    row_w = jnp.where(live, wts[assign], 0.0)
    dest_sorted = pstart[e_s] + jnp.arange(n_assign, dtype=jnp.int32) - start[e_s]
    inv = jnp.argsort(order).astype(jnp.int32)
    dest = dest_sorted[inv]
    n_valid = (pend[N_EXPERTS - 1] // tm).astype(jnp.int32).reshape(1)
    return row_tok, row_w, blk_e, n_valid, dest


def _final_body(x1_ref, m0_ref, m1_ref, m2_ref, m3_ref, g2_ref, lg_ref, lb_ref, o_ref, *, dn_alpha):
    moe = (m0_ref[...] + m1_ref[...]) + (m2_ref[...] + m3_ref[...])
    y = dn_alpha * x1_ref[...] + (1.0 + g2_ref[...]) * moe
    o_ref[...] = _layer_norm(y, lg_ref[...], lb_ref[...])


def _final_norm(x1, moe_parts, g2, ln_g, ln_b, tm, dn_alpha):
    b, t, _ = x1.shape
    per_row = g2.shape[1] != 1
    mod_spec = (pl.BlockSpec((None, tm, D_MODEL), lambda bi, i: (bi, i, 0)) if per_row
                else pl.BlockSpec((None, 1, D_MODEL), lambda bi, i: (bi, 0, 0)))
    blk = pl.BlockSpec((None, tm, D_MODEL), lambda bi, i: (bi, i, 0))
    const2 = pl.BlockSpec((1, D_MODEL), lambda bi, i: (0, 0))
    assert len(moe_parts) == TOP_K
    return pl.pallas_call(
        functools.partial(_final_body, dn_alpha=dn_alpha),
        grid=(b, t // tm),
        in_specs=[blk] * (1 + TOP_K) + [mod_spec, const2, const2],
        out_specs=blk,
        out_shape=jax.ShapeDtypeStruct((b, t, D_MODEL), F32),
        compiler_params=pltpu.CompilerParams(vmem_limit_bytes=VMEM_LIMIT),
        name="final_norm",
    )(x1, *moe_parts, g2, ln_g, ln_b)


def _layer(x_prompt, x_sample, c_prompt, c_sample, cache_k, cache_v, conv_state, ssm_state, past_len, depth,
           w_ada, b_ada, w_in, conv_w, a_log, dt_bias, gdn_norm_w, w_out, ln1_g, ln1_b,
           router_w, router_b, w_gu, b_gu, w_down, b_down, ln2_g, ln2_b):
    bp, tp, _ = x_prompt.shape
    db, ts, _ = x_sample.shape
    dn_alpha = (2.0 * depth) ** 0.25
    n_s = db * ts
    tm = 512
    row2 = lambda a: a.reshape(1, -1)

    mod = _ada_mod(jnp.concatenate([c_prompt, c_sample], axis=0), w_ada, b_ada)
    mod_p = [m[:, None, :] for m in jnp.split(mod[:bp], 6, axis=-1)]
    mod_s = [jnp.repeat(m, ts, axis=0)[None] for m in jnp.split(mod[bp:], 6, axis=-1)]

    w_in_b = jnp.pad(w_in, ((0, 0), (0, N_IN_PAD - N_IN))).astype(BF16)
    tab_p = _rope_tables(jnp.arange(tp, dtype=jnp.int32))
    tab_s = _rope_tables(jnp.tile(past_len + jnp.arange(ts, dtype=jnp.int32), db))
    xs_flat = x_sample.reshape(1, n_s, D_MODEL)

    qp, kp, vp, cin_p, z_p, ab_p = _in_proj(x_prompt, mod_p[1], mod_p[0], tab_p, w_in_b, tm)
    qs, ks, vs, cin_s, z_s, ab_s = _in_proj(xs_flat, mod_s[1], mod_s[0], tab_s, w_in_b, tm)

    oa_p = _attn_prompt(qp, kp, vp)
    keep = min(W_MAX, tp)
    new_k_p = kp[:, tp - keep:].reshape(bp, keep, A_HEADS, HEAD_DIM)
    new_v_p = vp[:, tp - keep:].reshape(bp, keep, A_HEADS, HEAD_DIM)
    oa_s, new_k_s, new_v_s = _attn_sample(
        qs.reshape(db, ts, A_WIDTH), ks.reshape(db, ts, A_WIDTH), vs.reshape(db, ts, A_WIDTH),
        cache_k.transpose(0, 2, 3, 1), cache_v.transpose(0, 2, 3, 1))
    new_k_s = new_k_s.transpose(0, 3, 1, 2)
    new_v_s = new_v_s.transpose(0, 3, 1, 2)

    ob_p, new_conv_p, new_ssm_p = _gdn(
        cin_p, z_p, ab_p, conv_w, jnp.zeros((bp, CONV_W - 1, CONV_DIM), F32),
        jnp.zeros((bp, B_HEADS, B_DK, B_DV), F32), a_log, dt_bias, gdn_norm_w, chunk=GDN_CHUNK)
    pad_t = lambda a: jnp.pad(a.reshape(db, ts, a.shape[-1]), ((0, 0), (0, GDN_CHUNK_SAMPLE - ts), (0, 0)))
    ob_s, new_conv_s, new_ssm_s = _gdn(pad_t(cin_s), pad_t(z_s), pad_t(ab_s), conv_w, conv_state, ssm_state,
                                       a_log, dt_bias, gdn_norm_w, chunk=GDN_CHUNK_SAMPLE, t_valid=ts)
    ob_s = ob_s[:, :ts]

    w_out_b = w_out.astype(BF16)
    rw_pad = jnp.pad(router_w, ((0, 0), (0, LANES - N_EXPERTS)))
    rb_pad = jnp.pad(router_b, (0, LANES - N_EXPERTS), constant_values=NEG).reshape(1, LANES)
    x1_p, h2_p, ti_p, tw_p = _out_proj(oa_p, ob_p, x_prompt, mod_p[2], mod_p[4], mod_p[3], w_out_b,
                                       row2(ln1_g), row2(ln1_b), rw_pad, rb_pad, tm, dn_alpha)
    x1_s, h2_s, ti_s, tw_s = _out_proj(oa_s.astype(BF16).reshape(1, n_s, A_WIDTH), ob_s.reshape(1, n_s, B_WIDTH), xs_flat,
                                       mod_s[2], mod_s[4], mod_s[3], w_out_b,
                                       row2(ln1_g), row2(ln1_b), rw_pad, rb_pad, tm, dn_alpha)

    n_p = bp * tp
    n_tok = n_p + n_s
    h2 = jnp.concatenate([h2_p.reshape(n_p, D_MODEL), h2_s.reshape(n_s, D_MODEL)], axis=0)
    top_i = jnp.concatenate([ti_p.reshape(n_p, LANES), ti_s.reshape(n_s, LANES)], axis=0)[:, :TOP_K]
    top_w = jnp.concatenate([tw_p.reshape(n_p, LANES), tw_s.reshape(n_s, LANES)], axis=0)[:, :TOP_K]
    row_tok, row_w, blk_e, n_valid, dest = _route(top_i, top_w, n_tok)
    xs_sorted = jnp.take(h2, row_tok, axis=0)
    ys = _moe_experts(xs_sorted, blk_e, n_valid, row_w, w_gu.astype(BF16), b_gu, w_down.astype(BF16), b_down)
    dest = dest.reshape(n_tok, TOP_K)
    parts_p = [jnp.take(ys, dest[:n_p, kk], axis=0).reshape(bp, tp, D_MODEL) for kk in range(TOP_K)]
    parts_s = [jnp.take(ys, dest[n_p:, kk], axis=0).reshape(1, n_s, D_MODEL) for kk in range(TOP_K)]

    y_p = _final_norm(x1_p, parts_p, mod_p[5], row2(ln2_g), row2(ln2_b), tm, dn_alpha)
    y_s = _final_norm(x1_s, parts_s, mod_s[5], row2(ln2_g), row2(ln2_b), tm, dn_alpha)
    return (y_p, y_s.reshape(db, ts, D_MODEL), new_k_p, new_v_p, new_conv_p, new_ssm_p,
            new_k_s, new_v_s, new_conv_s, new_ssm_s)


def kernel(x_prompt, x_sample, c_prompt, c_sample, cache_a_k, cache_a_v, state_b_conv, state_b_ssm, w_ada, b_ada, w_in, conv_w, a_log, dt_bias, gdn_norm_w, w_out, ln1_g, ln1_b, router_w, router_b, w_gu, b_gu, w_down, b_down, ln2_g, ln2_b):
    depth = w_ada.shape[0]
    assert depth == 1
    past_len = 8192
    outs = _layer(x_prompt, x_sample, c_prompt, c_sample, cache_a_k[0], cache_a_v[0], state_b_conv[0],
                  state_b_ssm[0], past_len, depth,
                  w_ada[0], b_ada[0], w_in[0], conv_w[0], a_log[0], dt_bias[0], gdn_norm_w[0], w_out[0],
                  ln1_g[0], ln1_b[0], router_w[0], router_b[0], w_gu[0], b_gu[0], w_down[0], b_down[0],
                  ln2_g[0], ln2_b[0])
    y_p, y_s = outs[0], outs[1]
    return (y_p, y_s) + tuple(o[None] for o in outs[2:])
```

```python
import functools
import math

import jax
import jax.numpy as jnp
from jax import lax
from jax.experimental import pallas as pl
from jax.experimental.pallas import tpu as pltpu

F32 = jnp.float32
BF16 = jnp.bfloat16

D_MODEL = 1024
HEAD_DIM = 64
A_HEADS = 12
B_HEADS = 4
B_DK = 64
B_DV = 64
A_WIDTH = A_HEADS * HEAD_DIM
B_WIDTH = B_HEADS * B_DV
DILATED_BRANCHES = ((128, 1), (512, 4), (2048, 16))
W_MAX = 2048
BAND = 128
ATTN_SCALE = HEAD_DIM ** -0.5
ROPE_DIM = HEAD_DIM // 4
ROPE_THETA = 500000.0
CONV_W = 4
CONV_DIM = 2 * B_HEADS * B_DK + B_HEADS * B_DV
N_EXPERTS = 32
TOP_K = 4
D_FF = D_MODEL
SWIGLU_LIMIT = 7.0
SWIGLU_ALPHA = 1.702
LN_EPS = 1e-5
NORM_EPS = 1e-6
N_IN = 3 * A_WIDTH + CONV_DIM + B_WIDTH + 2 * B_HEADS

LANES = 128
SUBLANES = 8
N_IN_PAD = -(-N_IN // LANES) * LANES
GDN_CHUNK = 64
GDN_CHUNK_SAMPLE = 16
MOE_TM = 512
NEG = -1e30
VMEM_LIMIT = 56 * 1024 * 1024


def _dot(a, b):
    return jnp.dot(a, b, preferred_element_type=F32)


def _dot_nt(a, b):
    return lax.dot_general(a, b, (((1,), (1,)), ((), ())), preferred_element_type=F32)


def _dot_tn(a, b):
    return lax.dot_general(a, b, (((0,), (0,)), ((), ())), preferred_element_type=F32)


def _split2(x):
    hi = x.astype(BF16)
    lo = (x - hi.astype(F32)).astype(BF16)
    return hi, lo


def _split3(x):
    hi = x.astype(BF16)
    r = x - hi.astype(F32)
    mid = r.astype(BF16)
    lo = (r - mid.astype(F32)).astype(BF16)
    return hi, mid, lo


def _dot3(a, b, dot=_dot):
    ah, al = _split2(a)
    bh, bl = _split2(b)
    return dot(ah, bh) + (dot(ah, bl) + dot(al, bh))


def _dot_sel_rhs(a, sel_bf16):
    hi, lo = _split2(a)
    return _dot(hi, sel_bf16) + _dot(lo, sel_bf16)


def _dot_sel_lhs(sel_bf16, b):
    hi, lo = _split2(b)
    return _dot(sel_bf16, hi) + _dot(sel_bf16, lo)


def _sigmoid(x):
    return 1.0 / (1.0 + jnp.exp(-x))


def _softplus(x):
    return jnp.maximum(x, 0.0) + jnp.log(1.0 + jnp.exp(-jnp.abs(x)))


def _layer_norm(y, g, b):
    mu = jnp.mean(y, axis=-1, keepdims=True)
    yc = y - mu
    var = jnp.mean(yc * yc, axis=-1, keepdims=True)
    return yc * lax.rsqrt(var + LN_EPS) * g + b


def _ada_body(c_ref, w_ref, b_ref, o_ref):
    c = c_ref[...]
    s = c * _sigmoid(c)
    o_ref[...] = _dot3(s, w_ref[...]) + b_ref[...]


def _ada_mod(c, w_ada, b_ada):
    n = c.shape[0]
    n_pad = -(-n // SUBLANES) * SUBLANES
    c = jnp.pad(c, ((0, n_pad - n), (0, 0)))
    tn = 512
    out = pl.pallas_call(
        _ada_body,
        grid=(w_ada.shape[1] // tn,),
        in_specs=[pl.BlockSpec((n_pad, D_MODEL), lambda j: (0, 0)),
                  pl.BlockSpec((D_MODEL, tn), lambda j: (0, j)),
                  pl.BlockSpec((1, tn), lambda j: (0, j))],
        out_specs=pl.BlockSpec((n_pad, tn), lambda j: (0, j)),
        out_shape=jax.ShapeDtypeStruct((n_pad, w_ada.shape[1]), F32),
        name="ada_mod",
    )(c, w_ada, b_ada.reshape(1, -1))
    return out[:n]


def _rope_tables(pos):
    half = ROPE_DIM // 2
    inv = ROPE_THETA ** (-jnp.arange(half, dtype=F32) / half)
    ang = pos.astype(F32)[:, None] * inv[None, :]
    cos, sin = jnp.cos(ang), jnp.sin(ang)
    t = pos.shape[0]
    pad = jnp.zeros((t, HEAD_DIM - ROPE_DIM), F32)
    zero = jnp.zeros((t, half), F32)
    ct = jnp.concatenate([cos, cos, pad + 1.0], axis=1)
    s1 = jnp.concatenate([-sin, zero, pad], axis=1)
    s2 = jnp.concatenate([zero, sin, pad], axis=1)
    rep = LANES // HEAD_DIM
    return tuple(jnp.tile(a, (1, rep)) for a in (ct, s1, s2))


def _inproj_body(x_ref, sc_ref, sh_ref, ct_ref, s1_ref, s2_ref, w_ref,
                 q_ref, k_ref, v_ref, c_ref, z_ref, ab_ref):
    h = x_ref[...] * (1.0 + sc_ref[...]) + sh_ref[...]
    hb = h.astype(BF16)
    ct, s1, s2 = ct_ref[...], s1_ref[...], s2_ref[...]

    def rope(t):
        return t * ct + pltpu.roll(t, LANES - ROPE_DIM // 2, 1) * s1 + pltpu.roll(t, ROPE_DIM // 2, 1) * s2

    for j in range(A_WIDTH // LANES):
        cols = slice(LANES * j, LANES * (j + 1))
        q_ref[:, cols] = rope(_dot(hb, w_ref[:, LANES * j:LANES * (j + 1)])) * ATTN_SCALE
        k_ref[:, cols] = rope(_dot(hb, w_ref[:, A_WIDTH + LANES * j:A_WIDTH + LANES * (j + 1)]))
    o1 = 3 * A_WIDTH
    o2 = o1 + CONV_DIM
    o3 = o2 + B_WIDTH
    v_ref[...] = _dot(hb, w_ref[:, 2 * A_WIDTH:o1])
    c_ref[...] = _dot(hb, w_ref[:, o1:o2])
    z_ref[...] = _dot(hb, w_ref[:, o2:o3])
    ab_ref[...] = _dot(hb, w_ref[:, o3:N_IN_PAD])


def _in_proj(x, sc, sh, tables, w_pad, tm):
    b, t, _ = x.shape
    per_row = sc.shape[1] != 1
    mod_spec = (pl.BlockSpec((None, tm, D_MODEL), lambda bi, i: (bi, i, 0)) if per_row
                else pl.BlockSpec((None, 1, D_MODEL), lambda bi, i: (bi, 0, 0)))
    tab_spec = pl.BlockSpec((tm, LANES), lambda bi, i: (i, 0))

    def out(width):
        return (pl.BlockSpec((None, tm, width), lambda bi, i: (bi, i, 0)),
                jax.ShapeDtypeStruct((b, t, width), F32))

    outs = [out(A_WIDTH), out(A_WIDTH), out(A_WIDTH), out(CONV_DIM), out(B_WIDTH), out(LANES)]
    return pl.pallas_call(
        _inproj_body,
        grid=(b, t // tm),
        in_specs=[pl.BlockSpec((None, tm, D_MODEL), lambda bi, i: (bi, i, 0)), mod_spec, mod_spec,
                  tab_spec, tab_spec, tab_spec,
                  pl.BlockSpec((D_MODEL, N_IN_PAD), lambda bi, i: (0, 0))],
        out_specs=[o[0] for o in outs],
        out_shape=[o[1] for o in outs],
        compiler_params=pltpu.CompilerParams(vmem_limit_bytes=VMEM_LIMIT),
        name="in_proj",
    )(x, sc, sh, *tables, w_pad)


ATTN_TQ = 2048
ATTN_GROUP = 4


def _attn_body(q_ref, kc_ref, kp_ref, vc_ref, vp_ref, o_ref, kbuf, vbuf, acc, mst, lst, bias):
    tq = ATTN_TQ
    first_tile = pl.program_id(1) == 0
    kbuf[0:tq, :] = kp_ref[...]
    kbuf[tq:2 * tq, :] = kc_ref[...]
    vbuf[0:tq, :] = vp_ref[...]
    vbuf[tq:2 * tq, :] = vc_ref[...]

    qi = lax.broadcasted_iota(jnp.int32, (BAND, 2 * BAND), 0)
    ki = lax.broadcasted_iota(jnp.int32, (BAND, 2 * BAND), 1)
    dist = qi + BAND - ki
    bias[0] = jnp.where((dist >= 0) & (dist <= BAND), 0.0, NEG)
    bias[1] = jnp.where(ki >= BAND, 0.0, NEG)
    lo = lax.broadcasted_iota(jnp.int32, (BAND, LANES), 1) < HEAD_DIM

    for r, (window, d) in enumerate(DILATED_BRANCHES):
        assert window // d == BAND
        span = BAND * d

        def blk_group(grp, carry, r=r, d=d, span=span):
            q_idx, kb, vb, mask, qh = {}, {}, {}, {}, {}
            for u in range(ATTN_GROUP):
                idx = grp * ATTN_GROUP + u
                c = idx % d
                j = idx // d
                qs = j * span + c
                ks = tq + qs - span
                if d == 1:
                    qs = pl.multiple_of(qs, BAND)
                    ks = pl.multiple_of(ks, BAND)
                    q_idx[u], k_idx = pl.ds(qs, BAND), pl.ds(ks, 2 * BAND)
                else:
                    q_idx[u], k_idx = pl.ds(qs, BAND, stride=d), pl.ds(ks, 2 * BAND, stride=d)
                qb = q_ref[q_idx[u], :]
                kb[u] = kbuf[k_idx, :].astype(BF16)
                vb[u] = vbuf[k_idx, :].astype(BF16)
                no_prev = jnp.where(first_tile & (j == 0), 1.0, 0.0)
                mask[u] = bias[0] + bias[1] * no_prev
                qh[u, 0] = jnp.where(lo, qb, 0.0).astype(BF16)
                qh[u, 1] = jnp.where(lo, 0.0, qb).astype(BF16)
            chains = [(u, hh) for u in range(ATTN_GROUP) for hh in range(2)]
            s = {ch: _dot_nt(qh[ch], kb[ch[0]]) + mask[ch[0]] for ch in chains}
            m = {ch: jnp.max(s[ch], axis=-1, keepdims=True) for ch in chains}
            p = {ch: jnp.exp(s[ch] - m[ch]) for ch in chains}
            l = {ch: jnp.sum(p[ch], axis=-1, keepdims=True) for ch in chains}
            o = {ch: _dot(p[ch].astype(BF16), vb[ch[0]]) for ch in chains}
            for u in range(ATTN_GROUP):
                acc[r, q_idx[u], :] = jnp.where(lo, o[u, 0], o[u, 1])
                mst[r, q_idx[u], :] = jnp.where(lo, m[u, 0], m[u, 1])
                lst[r, q_idx[u], :] = jnp.where(lo, l[u, 0], l[u, 1])
            return carry

        lax.fori_loop(0, tq // (BAND * ATTN_GROUP), blk_group, 0)

    def combine(t, carry):
        rows = pl.ds(pl.multiple_of(t * BAND, BAND), BAND)
        ms = [mst[r, rows, :] for r in range(3)]
        m = jnp.maximum(jnp.maximum(ms[0], ms[1]), ms[2])
        ws = [jnp.exp(mr - m) for mr in ms]
        num = ws[0] * acc[0, rows, :] + ws[1] * acc[1, rows, :] + ws[2] * acc[2, rows, :]
        den = ws[0] * lst[0, rows, :] + ws[1] * lst[1, rows, :] + ws[2] * lst[2, rows, :]
        o_ref[rows, :] = (num / den).astype(o_ref.dtype)
        return carry

    lax.fori_loop(0, tq // BAND, combine, 0)


def _attn_prompt(q, k, v):
    b, t, _ = q.shape
    tq = ATTN_TQ
    assert t % tq == 0
    cur = pl.BlockSpec((None, tq, LANES), lambda bi, i, hp: (bi, i, hp))
    prev = pl.BlockSpec((None, tq, LANES), lambda bi, i, hp: (bi, jnp.maximum(i - 1, 0), hp))
    return pl.pallas_call(
        _attn_body,
        grid=(b, t // tq, A_WIDTH // LANES),
        in_specs=[cur, cur, prev, cur, prev],
        out_specs=cur,
        out_shape=jax.ShapeDtypeStruct((b, t, A_WIDTH), BF16),
        scratch_shapes=[pltpu.VMEM((2 * tq, LANES), F32), pltpu.VMEM((2 * tq, LANES), F32),
                        pltpu.VMEM((3, tq, LANES), F32), pltpu.VMEM((3, tq, LANES), F32),
                        pltpu.VMEM((3, tq, LANES), F32), pltpu.VMEM((2, BAND, 2 * BAND), F32)],
        compiler_params=pltpu.CompilerParams(vmem_limit_bytes=VMEM_LIMIT),
        name="attn_prompt",
    )(q, k, k, v, v)


def _branch_count(delta):
    cnt = jnp.zeros(delta.shape, F32)
    for window, d in DILATED_BRANCHES:
        hit = (delta >= 0) & (delta <= window) & ((delta & (d - 1)) == 0)
        cnt = cnt + jnp.where(hit, 1.0, 0.0)
    return cnt


def _attn_sample_body(q_ref, kn_ref, vn_ref, ck_hbm, cv_hbm, o_ref, ok_hbm, ov_hbm,
                      inb, outb, p_cache, p_new, placed, in_sem, out_sem, *, a_buf, t_new):
    b = pl.program_id(0)
    w = pl.program_id(1)
    nb = pl.num_programs(0)
    srcs = (ck_hbm, cv_hbm)
    dsts = (ok_hbm, ov_hbm)
    n_tiles = a_buf // LANES
    new0 = LANES - t_new

    def in_copy(ws, bi):
        return pltpu.make_async_copy(srcs[ws].at[bi], inb.at[ws], in_sem.at[ws])

    def out_copy(ws, bi):
        return pltpu.make_async_copy(outb.at[ws], dsts[ws].at[bi], out_sem.at[ws])

    lane = lax.broadcasted_iota(jnp.int32, (t_new, LANES), 1)
    qrow = lax.broadcasted_iota(jnp.int32, (t_new, LANES), 0)
    place = jnp.where(lane == new0 + qrow, 1.0, 0.0).astype(BF16)

    def place_new(x_ref):
        hi, mid, lo = _split3(x_ref[...])
        xt = _dot_tn(hi, place) + (_dot_tn(mid, place) + _dot_tn(lo, place))
        return xt.reshape(A_HEADS, HEAD_DIM, LANES)

    def shift_into(ws):
        keep = lax.broadcasted_iota(jnp.int32, (HEAD_DIM, LANES), 1) < new0

        def per_head(h, carry):
            prev = pltpu.roll(inb[ws, h, :, 0:LANES], new0, 1)
            for j in range(n_tiles):
                if j + 1 < n_tiles:
                    nxt = pltpu.roll(inb[ws, h, :, (j + 1) * LANES:(j + 2) * LANES], new0, 1)
                else:
                    nxt = placed[h]
                outb[ws, h, :, j * LANES:(j + 1) * LANES] = jnp.where(keep, prev, nxt)
                prev = nxt
            return carry

        lax.fori_loop(0, A_HEADS, per_head, 0)

    @pl.when(w == 0)
    def _():
        @pl.when(b == 0)
        def _():
            in_copy(0, 0).start()

        in_copy(1, b).start()
        placed[...] = place_new(kn_ref)
        in_copy(0, b).wait()

        pos = lax.broadcasted_iota(jnp.int32, (t_new, a_buf), 1)
        cnt = _branch_count(a_buf + lax.broadcasted_iota(jnp.int32, (t_new, a_buf), 0) - pos)
        cnt_new = jnp.where(lane >= new0, _branch_count(qrow - (lane - new0)), 0.0)
        q_all = q_ref[...]
        for h in range(A_HEADS):
            qh = q_all[:, h * HEAD_DIM:(h + 1) * HEAD_DIM].astype(BF16)
            s = jnp.where(cnt > 0.0, _dot(qh, inb[0, h].astype(BF16)), NEG)
            sn = jnp.where(cnt_new > 0.0, _dot(qh, placed[h].astype(BF16)), NEG)
            m = jnp.maximum(jnp.max(s, axis=-1, keepdims=True), jnp.max(sn, axis=-1, keepdims=True))
            p = cnt * jnp.exp(s - m)
            pn = cnt_new * jnp.exp(sn - m)
            inv_l = 1.0 / (jnp.sum(p, axis=-1, keepdims=True) + jnp.sum(pn, axis=-1, keepdims=True))
            p_cache[h] = p * inv_l
            p_new[h] = pn * inv_l

        @pl.when(b > 0)
        def _():
            out_copy(0, b - 1).wait()

        shift_into(0)
        out_copy(0, b).start()

    @pl.when(w == 1)
    def _():
        @pl.when(b + 1 < nb)
        def _():
            in_copy(0, b + 1).start()

        placed[...] = place_new(vn_ref)
        in_copy(1, b).wait()
        for h in range(A_HEADS):
            o_h = (_dot_nt(p_cache[h].astype(BF16), inb[1, h].astype(BF16))
                   + _dot_nt(p_new[h].astype(BF16), placed[h].astype(BF16)))
            o_ref[:, h * HEAD_DIM:(h + 1) * HEAD_DIM] = o_h

        @pl.when(b > 0)
        def _():
            out_copy(1, b - 1).wait()

        shift_into(1)
        out_copy(1, b).start()

        @pl.when(b == nb - 1)
        def _():
            out_copy(0, b).wait()
            out_copy(1, b).wait()


def _attn_sample(q, k_new, v_new, cache_kt, cache_vt):
    db, t_new, _ = q.shape
    a_buf = cache_kt.shape[-1]
    assert t_new == SUBLANES and a_buf == W_MAX and cache_kt.shape[1:3] == (A_HEADS, HEAD_DIM)
    row = pl.BlockSpec((None, t_new, A_WIDTH), lambda bi, wi: (bi, 0, 0))
    hbm = pl.BlockSpec(memory_space=pl.ANY)
    body = functools.partial(_attn_sample_body, a_buf=a_buf, t_new=t_new)
    return pl.pallas_call(
        body,
        grid=(db, 2),
        in_specs=[row, row, row, hbm, hbm],
        out_specs=[row, hbm, hbm],
        out_shape=[jax.ShapeDtypeStruct((db, t_new, A_WIDTH), F32),
                   jax.ShapeDtypeStruct(cache_kt.shape, F32), jax.ShapeDtypeStruct(cache_vt.shape, F32)],
        scratch_shapes=[pltpu.VMEM((2, A_HEADS, HEAD_DIM, a_buf), F32), pltpu.VMEM((2, A_HEADS, HEAD_DIM, a_buf), F32),
                        pltpu.VMEM((A_HEADS, t_new, a_buf), F32), pltpu.VMEM((A_HEADS, t_new, LANES), F32),
                        pltpu.VMEM((A_HEADS, HEAD_DIM, LANES), F32),
                        pltpu.SemaphoreType.DMA((2,)), pltpu.SemaphoreType.DMA((2,))],
        compiler_params=pltpu.CompilerParams(vmem_limit_bytes=VMEM_LIMIT,
                                             dimension_semantics=("arbitrary", "arbitrary")),
        name="attn_sample",
    )(q, k_new, v_new, cache_kt, cache_vt)


def _gdn_body(xc_ref, z_ref, ab_ref, cw_ref, cs_ref, s0_ref, al_ref, dtb_ref, nw_ref,
              o_ref, nc_ref, ns_ref,
              xbuf, state, qs, ks, vs, bs, gs, *, bb, tc, c, t_valid, unroll):
    g = pl.program_id(1)
    hd = B_DK
    nh = B_HEADS
    width = nh * hd

    @pl.when(g == 0)
    def _():
        xbuf[:, 0:SUBLANES, :] = cs_ref[...]
        state[...] = s0_ref[...]

    grp_r = lax.broadcasted_iota(jnp.int32, (width, width), 0) // hd
    grp_c = lax.broadcasted_iota(jnp.int32, (width, width), 1) // hd
    head_sum = jnp.where(grp_r == grp_c, 1.0, 0.0).astype(BF16)
    src = lax.broadcasted_iota(jnp.int32, (LANES, width), 0)
    dst = lax.broadcasted_iota(jnp.int32, (LANES, width), 1) // hd
    pick_a = jnp.where(src == dst, 1.0, 0.0).astype(BF16)
    pick_b = jnp.where(src == dst + nh, 1.0, 0.0).astype(BF16)

    for bi in range(bb):
        xbuf[bi, SUBLANES:SUBLANES + tc, :] = xc_ref[bi]
        y = xbuf[bi, pl.ds(SUBLANES - (CONV_W - 1), tc), :] * cw_ref[0:1, :]
        for j in range(1, CONV_W):
            y = y + xbuf[bi, pl.ds(SUBLANES - (CONV_W - 1) + j, tc), :] * cw_ref[j:j + 1, :]
        y = y * _sigmoid(y)
        tail = xbuf[bi, t_valid:t_valid + SUBLANES, :]
        xbuf[bi, 0:SUBLANES, :] = tail
        nc_ref[bi] = tail

        q = y[:, 0:width]
        k = y[:, width:2 * width]
        qs[bi] = q * lax.rsqrt(_dot_sel_rhs(q * q, head_sum) + NORM_EPS) * (hd ** -0.5)
        ks[bi] = k * lax.rsqrt(_dot_sel_rhs(k * k, head_sum) + NORM_EPS)
        vs[bi] = y[:, 2 * width:3 * width]
        ab = ab_ref[bi]
        beta = _sigmoid(_dot_sel_rhs(ab, pick_b))
        gate = -jnp.exp(al_ref[...]) * _softplus(_dot_sel_rhs(ab, pick_a) + dtb_ref[...])
        if t_valid < tc:
            live = lax.broadcasted_iota(jnp.int32, (tc, width), 0) < t_valid
            beta = jnp.where(live, beta, 0.0)
            gate = jnp.where(live, gate, 0.0)
        bs[bi] = beta
        gs[bi] = gate

    ri = lax.broadcasted_iota(jnp.int32, (c, c), 0)
    ci = lax.broadcasted_iota(jnp.int32, (c, c), 1)
    tri_incl = ri >= ci
    tri_strict = ri > ci
    tri_incl_b = jnp.where(tri_incl, 1.0, 0.0).astype(BF16)
    eye = jnp.where(ri == ci, 1.0, 0.0)
    ones_b = jnp.ones((c, c), BF16)
    n_sq = int(math.log2(c)) - 2

    n_sub = unroll

    def chunk_group(cg, carry):
        seqs = [(s, bi) for s in range(n_sub) for bi in range(bb)]
        pre = {}
        for s, bi in seqs:
            rows = pl.ds(pl.multiple_of((cg * n_sub + s) * c, c), c)
            qc, kc, vc, bc = qs[bi, rows, :], ks[bi, rows, :], vs[bi, rows, :], bs[bi, rows, :]
            gcum = _dot_sel_lhs(tri_incl_b, gs[bi, rows, :])
            glast = gcum[c - 1:c, :]
            eg = jnp.exp(gcum)
            kb = kc * bc
            pre[s, bi] = dict(rows=rows, gcum=gcum, glast=glast, kcb=kc.astype(BF16), kbb=kb.astype(BF16),
                              qcb=qc.astype(BF16), vbb=(vc * bc).astype(BF16), wkb=(kb * eg).astype(BF16),
                              qgb=(qc * eg).astype(BF16), kdb=(kc * jnp.exp(glast - gcum)).astype(BF16))
        chains = [(s, bi, h) for s, bi in seqs for h in range(nh)]
        sl = lambda h: slice(h * hd, (h + 1) * hd)
        g_col = {ch: pre[ch[:2]]['gcum'][:, ch[2] * hd:ch[2] * hd + c] for ch in chains}
        g_row = {ch: _dot_sel_lhs(ones_b, eye * g_col[ch]) for ch in chains}
        decay = {ch: jnp.where(tri_incl, jnp.exp(g_col[ch] - g_row[ch]), 0.0) for ch in chains}
        kk = {ch: _dot_nt(pre[ch[:2]]['kbb'][:, sl(ch[2])], pre[ch[:2]]['kcb'][:, sl(ch[2])]) for ch in chains}
        qk = {ch: _dot_nt(pre[ch[:2]]['qcb'][:, sl(ch[2])], pre[ch[:2]]['kcb'][:, sl(ch[2])]) for ch in chains}
        attn = {ch: (qk[ch] * decay[ch]).astype(BF16) for ch in chains}
        neg = {ch: jnp.where(tri_strict, -(kk[ch] * decay[ch]), 0.0) for ch in chains}
        tinv = {ch: eye + neg[ch] for ch in chains}
        negb = {ch: neg[ch].astype(BF16) for ch in chains}
        pw = {ch: _dot(negb[ch], negb[ch]) for ch in chains}
        tinv = {ch: tinv[ch] + _dot(tinv[ch].astype(BF16), pw[ch].astype(BF16)) for ch in chains}
        for _ in range(n_sq):
            pwb = {ch: pw[ch].astype(BF16) for ch in chains}
            pw = {ch: _dot(pwb[ch], pwb[ch]) for ch in chains}
            tinv = {ch: tinv[ch] + _dot(tinv[ch].astype(BF16), pw[ch].astype(BF16)) for ch in chains}
        tb = {ch: tinv[ch].astype(BF16) for ch in chains}
        u = {ch: _dot(tb[ch], pre[ch[:2]]['vbb'][:, sl(ch[2])]) for ch in chains}
        w = {ch: _dot(tb[ch], pre[ch[:2]]['wkb'][:, sl(ch[2])]).astype(BF16) for ch in chains}
        for s in range(n_sub):
            live = [ch for ch in chains if ch[0] == s]
            s_old = {ch: state[ch[1], ch[2]] for ch in live}
            s_b = {ch: s_old[ch].astype(BF16) for ch in live}
            ws = {ch: _dot(w[ch], s_b[ch]) for ch in live}
            qs_ = {ch: _dot(pre[ch[:2]]['qgb'][:, sl(ch[2])], s_b[ch]) for ch in live}
            v_new = {ch: (u[ch] - ws[ch]).astype(BF16) for ch in live}
            av = {ch: _dot(attn[ch], v_new[ch]) for ch in live}
            kv = {ch: _dot_tn(pre[ch[:2]]['kdb'][:, sl(ch[2])], v_new[ch]) for ch in live}
            for ch in live:
                state[ch[1], ch[2]] = s_old[ch] * jnp.exp(pre[ch[:2]]['glast'][:, sl(ch[2])]) + kv[ch]
            for bi in range(bb):
                outs = []
                for h in range(nh):
                    o_h = qs_[s, bi, h] + av[s, bi, h]
                    outs.append(o_h * lax.rsqrt(jnp.mean(o_h * o_h, axis=-1, keepdims=True) + NORM_EPS))
                rows = pre[s, bi]['rows']
                zc = z_ref[bi, rows, :]
                o_ref[bi, rows, :] = (jnp.concatenate(outs, axis=1) * nw_ref[...]
                                      * (zc * _sigmoid(zc))).astype(o_ref.dtype)
        return carry

    lax.fori_loop(0, tc // (c * n_sub), chunk_group, 0)
    ns_ref[...] = state[...]


def _gdn(conv_in, z, ab, conv_w, conv_state, ssm0, a_log, dt_bias, norm_w, *, chunk, t_valid=None):
    b, t, _ = conv_in.shape
    c = chunk
    bb = 2
    tc = min(t, 4 * c)
    assert t % tc == 0 and tc % c == 0 and b % bb == 0 and c <= B_DK
    t_valid = t if t_valid is None else t_valid
    assert t_valid == t or t == tc
    n_chunks = tc // c
    rep = lambda a: jnp.repeat(a.astype(F32), B_DK).reshape(1, B_WIDTH)
    blk = lambda w: pl.BlockSpec((bb, tc, w), lambda bi, gi: (bi, gi, 0))
    const2 = lambda shape: pl.BlockSpec(shape, lambda bi, gi: (0, 0))
    body = functools.partial(_gdn_body, bb=bb, tc=tc, c=c, t_valid=min(t_valid, tc),
                             unroll=2 if n_chunks % 2 == 0 else 1)
    tail_spec = pl.BlockSpec((bb, SUBLANES, CONV_DIM), lambda bi, gi: (bi, 0, 0))
    state_spec = pl.BlockSpec((bb, B_HEADS, B_DK, B_DV), lambda bi, gi: (bi, 0, 0, 0))
    conv_tail = jnp.pad(conv_state, ((0, 0), (SUBLANES - (CONV_W - 1), 0), (0, 0)))
    o, new_tail, new_state = pl.pallas_call(
        body,
        grid=(b // bb, t // tc),
        in_specs=[blk(CONV_DIM), blk(B_WIDTH), blk(LANES), const2((CONV_W, CONV_DIM)),
                  tail_spec, state_spec,
                  const2((1, B_WIDTH)), const2((1, B_WIDTH)), const2((1, B_WIDTH))],
        out_specs=[blk(B_WIDTH), tail_spec, state_spec],
        out_shape=[jax.ShapeDtypeStruct((b, t, B_WIDTH), BF16),
                   jax.ShapeDtypeStruct((b, SUBLANES, CONV_DIM), F32),
                   jax.ShapeDtypeStruct((b, B_HEADS, B_DK, B_DV), F32)],
        scratch_shapes=[pltpu.VMEM((bb, tc + SUBLANES, CONV_DIM), F32),
                        pltpu.VMEM((bb, B_HEADS, B_DK, B_DV), F32)]
        + [pltpu.VMEM((bb, tc, B_WIDTH), F32)] * 5,
        compiler_params=pltpu.CompilerParams(vmem_limit_bytes=VMEM_LIMIT,
                                             dimension_semantics=("arbitrary", "arbitrary")),
        name="gdn",
    )(conv_in, z, ab, conv_w, conv_tail, ssm0, rep(a_log), rep(dt_bias),
      jnp.tile(norm_w.astype(F32), B_HEADS).reshape(1, B_WIDTH))
    return o, new_tail[:, SUBLANES - (CONV_W - 1):], new_state


def _outproj_body(oa_ref, ob_ref, x_ref, g1_ref, sc2_ref, sh2_ref, w_ref, lg_ref, lb_ref, rw_ref, rb_ref,
                  x1_ref, h2_ref, ti_ref, tw_ref, cnt_ref, *, dn_alpha):
    mix = _dot(oa_ref[...], w_ref[0:A_WIDTH, :]) + _dot(ob_ref[...], w_ref[A_WIDTH:A_WIDTH + B_WIDTH, :])
    x1 = _layer_norm(dn_alpha * x_ref[...] + (1.0 + g1_ref[...]) * mix, lg_ref[...], lb_ref[...])
    x1_ref[...] = x1
    h2 = x1 * (1.0 + sc2_ref[...]) + sh2_ref[...]
    h2_ref[...] = h2.astype(BF16)
    logits = _dot3(h2, rw_ref[...]) + rb_ref[...]
    lane = lax.broadcasted_iota(jnp.int32, logits.shape, 1)
    ti = jnp.zeros(logits.shape, jnp.int32)
    tv = jnp.zeros(logits.shape, F32)
    picked = jnp.zeros(logits.shape, F32)
    vals = []
    for kk in range(TOP_K):
        m = jnp.max(logits, axis=-1, keepdims=True)
        idx = jnp.min(jnp.where(logits == m, lane.astype(F32), float(LANES)), axis=-1, keepdims=True).astype(jnp.int32)
        vals.append(m)
        ti = jnp.where(lane == kk, idx, ti)
        picked = jnp.where(lane == idx, 1.0, picked)
        logits = jnp.where(lane == idx, NEG, logits)
    es = [jnp.exp(v - vals[0]) for v in vals]
    den = es[0] + es[1] + es[2] + es[3]
    for kk in range(TOP_K):
        tv = jnp.where(lane == kk, es[kk] / den, tv)
    ti_ref[...] = ti
    tw_ref[...] = tv
    per_expert = jnp.sum(picked, axis=0, keepdims=True)
    row = lax.broadcasted_iota(jnp.int32, (SUBLANES, LANES), 0)
    cnt_ref[...] = jnp.where(row == 0, per_expert, 0.0)


def _out_proj(oa, ob, x, g1, sc2, sh2, w_out_b, ln_g, ln_b, rw_pad, rb_pad, tm, dn_alpha):
    b, t, _ = x.shape
    per_row = g1.shape[1] != 1
    mod_spec = (pl.BlockSpec((None, tm, D_MODEL), lambda bi, i: (bi, i, 0)) if per_row
                else pl.BlockSpec((None, 1, D_MODEL), lambda bi, i: (bi, 0, 0)))
    blk = lambda w: pl.BlockSpec((None, tm, w), lambda bi, i: (bi, i, 0))
    const2 = lambda shape: pl.BlockSpec(shape, lambda bi, i: (0, 0))
    return pl.pallas_call(
        functools.partial(_outproj_body, dn_alpha=dn_alpha),
        grid=(b, t // tm),
        in_specs=[blk(A_WIDTH), blk(B_WIDTH), blk(D_MODEL), mod_spec, mod_spec, mod_spec,
                  const2((D_MODEL, D_MODEL)), const2((1, D_MODEL)), const2((1, D_MODEL)),
                  const2((D_MODEL, LANES)), const2((1, LANES))],
        out_specs=[blk(D_MODEL), blk(D_MODEL), blk(LANES), blk(LANES),
                   pl.BlockSpec((None, SUBLANES, LANES), lambda bi, i: (bi, i, 0))],
        out_shape=[jax.ShapeDtypeStruct((b, t, D_MODEL), F32), jax.ShapeDtypeStruct((b, t, D_MODEL), BF16),
                   jax.ShapeDtypeStruct((b, t, LANES), jnp.int32), jax.ShapeDtypeStruct((b, t, LANES), F32),
                   jax.ShapeDtypeStruct((b, t // tm * SUBLANES, LANES), F32)],
        compiler_params=pltpu.CompilerParams(vmem_limit_bytes=VMEM_LIMIT),
        name="out_proj",
    )(oa, ob, x, g1, sc2, sh2, w_out_b, ln_g, ln_b, rw_pad, rb_pad)


def _moe_body(be_ref, nv_ref, x_ref, wgu_ref, bgu_ref, wd_ref, bd_ref, o_ref):
    i = pl.program_id(0)

    @pl.when(i < nv_ref[0])
    def _():
        hcat = _dot(x_ref[...], wgu_ref[...]) + bgu_ref[...]
        gate = jnp.minimum(hcat[:, :D_FF], SWIGLU_LIMIT)
        up = jnp.clip(hcat[:, D_FF:], -SWIGLU_LIMIT, SWIGLU_LIMIT)
        glu = gate * _sigmoid(SWIGLU_ALPHA * gate)
        act = (glu * (up + 1.0)).astype(BF16)
        o_ref[...] = _dot(act, wd_ref[...]) + bd_ref[...]

    @pl.when(i >= nv_ref[0])
    def _():
        o_ref[...] = jnp.zeros(o_ref.shape, o_ref.dtype)


def _moe_experts(xs, blk_e, n_valid, w_gu_b, b_gu, w_down_b, b_down):
    cap = xs.shape[0]
    tm = MOE_TM
    grid_spec = pltpu.PrefetchScalarGridSpec(
        num_scalar_prefetch=2,
        grid=(cap // tm,),
        in_specs=[pl.BlockSpec((tm, D_MODEL), lambda i, be, nv: (i, 0)),
                  pl.BlockSpec((None, D_MODEL, 2 * D_FF), lambda i, be, nv: (be[i], 0, 0)),
                  pl.BlockSpec((None, 1, 2 * D_FF), lambda i, be, nv: (be[i], 0, 0)),
                  pl.BlockSpec((None, D_FF, D_MODEL), lambda i, be, nv: (be[i], 0, 0)),
                  pl.BlockSpec((None, 1, D_MODEL), lambda i, be, nv: (be[i], 0, 0))],
        out_specs=pl.BlockSpec((tm, D_MODEL), lambda i, be, nv: (i, 0)),
    )
    return pl.pallas_call(
        _moe_body,
        grid_spec=grid_spec,
        out_shape=jax.ShapeDtypeStruct((cap, D_MODEL), F32),
        compiler_params=pltpu.CompilerParams(vmem_limit_bytes=VMEM_LIMIT),
        name="moe_experts",
    )(blk_e, n_valid, xs, w_gu_b, b_gu.reshape(N_EXPERTS, 1, 2 * D_FF), w_down_b,
      b_down.reshape(N_EXPERTS, 1, D_MODEL))


def _take_rows(x, idx):
    return x.at[idx].get(mode="promise_in_bounds")


def _route(eid, counts, n_tok):
    tm = MOE_TM
    n_assign = n_tok * TOP_K
    order = jnp.argsort(eid, stable=True).astype(jnp.int32)
    start = jnp.cumsum(counts) - counts
    padded = (counts + tm - 1) // tm * tm
    pstart = jnp.cumsum(padded) - padded
    pend = pstart + padded
    n_blocks = -(-(n_assign + N_EXPERTS * (tm - 1)) // tm)
    blk_first = jnp.arange(n_blocks, dtype=jnp.int32) * tm
    blk_e = jnp.minimum(jnp.sum((pend[None, :] <= blk_first[:, None]).astype(jnp.int32), axis=1), N_EXPERTS - 1)
    within = jnp.arange(tm, dtype=jnp.int32)[None, :]
    rank = (blk_first - pstart[blk_e])[:, None] + within
    live = (rank < counts[blk_e][:, None]) & (blk_first < pend[N_EXPERTS - 1])[:, None]
    src = jnp.clip(start[blk_e][:, None] + rank, 0, n_assign - 1)
    assign = _take_rows(order, src.reshape(-1))
    row_tok = jnp.where(live.reshape(-1), assign % n_tok, 0)
    shift = pstart - start
    step = jnp.concatenate([shift[:1], shift[1:] - shift[:-1]])
    pos = jnp.arange(n_assign, dtype=jnp.int32)
    dest_sorted = pos + jnp.sum(jnp.where(pos[:, None] >= start[None, :], step[None, :], 0), axis=1)
    inv = jnp.argsort(order).astype(jnp.int32)
    dest = _take_rows(dest_sorted, inv).reshape(TOP_K, n_tok)
    n_valid = (pend[N_EXPERTS - 1] // tm).astype(jnp.int32).reshape(1)
    return row_tok, blk_e.astype(jnp.int32), n_valid, dest


def _final_body(x1_ref, m0_ref, m1_ref, m2_ref, m3_ref, tw_ref, g2_ref, lg_ref, lb_ref, o_ref, *, dn_alpha):
    tw = tw_ref[...]
    moe = ((m0_ref[...] * tw[:, 0:1] + m1_ref[...] * tw[:, 1:2])
           + (m2_ref[...] * tw[:, 2:3] + m3_ref[...] * tw[:, 3:4]))
    y = dn_alpha * x1_ref[...] + (1.0 + g2_ref[...]) * moe
    o_ref[...] = _layer_norm(y, lg_ref[...], lb_ref[...])


def _final_norm(x1, moe_parts, top_w, g2, ln_g, ln_b, tm, dn_alpha):
    b, t, _ = x1.shape
    per_row = g2.shape[1] != 1
    mod_spec = (pl.BlockSpec((None, tm, D_MODEL), lambda bi, i: (bi, i, 0)) if per_row
                else pl.BlockSpec((None, 1, D_MODEL), lambda bi, i: (bi, 0, 0)))
    blk = pl.BlockSpec((None, tm, D_MODEL), lambda bi, i: (bi, i, 0))
    const2 = pl.BlockSpec((1, D_MODEL), lambda bi, i: (0, 0))
    assert len(moe_parts) == TOP_K
    return pl.pallas_call(
        functools.partial(_final_body, dn_alpha=dn_alpha),
        grid=(b, t // tm),
        in_specs=[blk] * (1 + TOP_K) + [pl.BlockSpec((None, tm, LANES), lambda bi, i: (bi, i, 0)),
                                        mod_spec, const2, const2],
        out_specs=blk,
        out_shape=jax.ShapeDtypeStruct((b, t, D_MODEL), F32),
        compiler_params=pltpu.CompilerParams(vmem_limit_bytes=VMEM_LIMIT),
        name="final_norm",
    )(x1, *moe_parts, top_w, g2, ln_g, ln_b)


def _layer(x_prompt, x_sample, c_prompt, c_sample, cache_k, cache_v, conv_state, ssm_state, past_len, depth,
           w_ada, b_ada, w_in, conv_w, a_log, dt_bias, gdn_norm_w, w_out, ln1_g, ln1_b,
           router_w, router_b, w_gu, b_gu, w_down, b_down, ln2_g, ln2_b):
    bp, tp, _ = x_prompt.shape
    db, ts, _ = x_sample.shape
    dn_alpha = (2.0 * depth) ** 0.25
    n_s = db * ts
    tm = 512
    row2 = lambda a: a.reshape(1, -1)

    mod = _ada_mod(jnp.concatenate([c_prompt, c_sample], axis=0), w_ada, b_ada)
    mod_p = [m[:, None, :] for m in jnp.split(mod[:bp], 6, axis=-1)]
    mod_s = [jnp.repeat(m, ts, axis=0)[None] for m in jnp.split(mod[bp:], 6, axis=-1)]

    w_in_b = jnp.pad(w_in, ((0, 0), (0, N_IN_PAD - N_IN))).astype(BF16)
    tab_p = _rope_tables(jnp.arange(tp, dtype=jnp.int32))
    tab_s = _rope_tables(jnp.tile(past_len + jnp.arange(ts, dtype=jnp.int32), db))
    xs_flat = x_sample.reshape(1, n_s, D_MODEL)

    qp, kp, vp, cin_p, z_p, ab_p = _in_proj(x_prompt, mod_p[1], mod_p[0], tab_p, w_in_b, tm)
    qs, ks, vs, cin_s, z_s, ab_s = _in_proj(xs_flat, mod_s[1], mod_s[0], tab_s, w_in_b, tm)

    oa_p = _attn_prompt(qp, kp, vp)
    keep = min(W_MAX, tp)
    new_k_p = kp[:, tp - keep:].reshape(bp, keep, A_HEADS, HEAD_DIM)
    new_v_p = vp[:, tp - keep:].reshape(bp, keep, A_HEADS, HEAD_DIM)
    oa_s, new_k_s, new_v_s = _attn_sample(
        qs.reshape(db, ts, A_WIDTH), ks.reshape(db, ts, A_WIDTH), vs.reshape(db, ts, A_WIDTH),
        cache_k.transpose(0, 2, 3, 1), cache_v.transpose(0, 2, 3, 1))
    new_k_s = new_k_s.transpose(0, 3, 1, 2)
    new_v_s = new_v_s.transpose(0, 3, 1, 2)

    ob_p, new_conv_p, new_ssm_p = _gdn(
        cin_p, z_p, ab_p, conv_w, jnp.zeros((bp, CONV_W - 1, CONV_DIM), F32),
        jnp.zeros((bp, B_HEADS, B_DK, B_DV), F32), a_log, dt_bias, gdn_norm_w, chunk=GDN_CHUNK)
    pad_t = lambda a: jnp.pad(a.reshape(db, ts, a.shape[-1]), ((0, 0), (0, GDN_CHUNK_SAMPLE - ts), (0, 0)))
    ob_s, new_conv_s, new_ssm_s = _gdn(pad_t(cin_s), pad_t(z_s), pad_t(ab_s), conv_w, conv_state, ssm_state,
                                       a_log, dt_bias, gdn_norm_w, chunk=GDN_CHUNK_SAMPLE, t_valid=ts)
    ob_s = ob_s[:, :ts]

    w_out_b = w_out.astype(BF16)
    rw_pad = jnp.pad(router_w, ((0, 0), (0, LANES - N_EXPERTS)))
    rb_pad = jnp.pad(router_b, (0, LANES - N_EXPERTS), constant_values=NEG).reshape(1, LANES)
    x1_p, h2_p, ti_p, tw_p, cnt_p = _out_proj(oa_p, ob_p, x_prompt, mod_p[2], mod_p[4], mod_p[3], w_out_b,
                                              row2(ln1_g), row2(ln1_b), rw_pad, rb_pad, tm, dn_alpha)
    x1_s, h2_s, ti_s, tw_s, cnt_s = _out_proj(oa_s.astype(BF16).reshape(1, n_s, A_WIDTH),
                                              ob_s.reshape(1, n_s, B_WIDTH), xs_flat,
                                              mod_s[2], mod_s[4], mod_s[3], w_out_b,
                                              row2(ln1_g), row2(ln1_b), rw_pad, rb_pad, tm, dn_alpha)

    n_p = bp * tp
    n_tok = n_p + n_s
    h2 = jnp.concatenate([h2_p.reshape(n_p, D_MODEL), h2_s.reshape(n_s, D_MODEL)], axis=0)
    top_i = jnp.concatenate([ti_p.reshape(n_p, LANES), ti_s.reshape(n_s, LANES)], axis=0)[:, :TOP_K]
    counts = (cnt_p.sum(axis=(0, 1)) + cnt_s.sum(axis=(0, 1)))[:N_EXPERTS].astype(jnp.int32)
    row_tok, blk_e, n_valid, dest = _route(top_i.T.reshape(-1), counts, n_tok)
    xs_sorted = _take_rows(h2, row_tok)
    ys = _moe_experts(xs_sorted, blk_e, n_valid, w_gu.astype(BF16), b_gu, w_down.astype(BF16), b_down)
    gather = lambda idx: _take_rows(ys, idx)
    parts_p = [gather(dest[kk, :n_p]).reshape(bp, tp, D_MODEL) for kk in range(TOP_K)]
    parts_s = [gather(dest[kk, n_p:]).reshape(1, n_s, D_MODEL) for kk in range(TOP_K)]

    y_p = _final_norm(x1_p, parts_p, tw_p, mod_p[5], row2(ln2_g), row2(ln2_b), tm, dn_alpha)
    y_s = _final_norm(x1_s, parts_s, tw_s, mod_s[5], row2(ln2_g), row2(ln2_b), tm, dn_alpha)
    return (y_p, y_s.reshape(db, ts, D_MODEL), new_k_p, new_v_p, new_conv_p, new_ssm_p,
            new_k_s, new_v_s, new_conv_s, new_ssm_s)


def kernel(x_prompt, x_sample, c_prompt, c_sample, cache_a_k, cache_a_v, state_b_conv, state_b_ssm, w_ada, b_ada, w_in, conv_w, a_log, dt_bias, gdn_norm_w, w_out, ln1_g, ln1_b, router_w, router_b, w_gu, b_gu, w_down, b_down, ln2_g, ln2_b):
    depth = w_ada.shape[0]
    assert depth == 1
    past_len = 8192
    outs = _layer(x_prompt, x_sample, c_prompt, c_sample, cache_a_k[0], cache_a_v[0], state_b_conv[0],
                  state_b_ssm[0], past_len, depth,
                  w_ada[0], b_ada[0], w_in[0], conv_w[0], a_log[0], dt_bias[0], gdn_norm_w[0], w_out[0],
                  ln1_g[0], ln1_b[0], router_w[0], router_b[0], w_gu[0], b_gu[0], w_down[0], b_down[0],
                  ln2_g[0], ln2_b[0])
    y_p, y_s = outs[0], outs[1]
    return (y_p, y_s) + tuple(o[None] for o in outs[2:])
```

```python
import functools
import math

import jax
import jax.numpy as jnp
from jax import lax
from jax.experimental import pallas as pl
from jax.experimental.pallas import tpu as pltpu

F32 = jnp.float32
BF16 = jnp.bfloat16

D_MODEL = 1024
HEAD_DIM = 64
A_HEADS = 12
B_HEADS = 4
B_DK = 64
B_DV = 64
A_WIDTH = A_HEADS * HEAD_DIM
B_WIDTH = B_HEADS * B_DV
DILATED_BRANCHES = ((128, 1), (512, 4), (2048, 16))
W_MAX = 2048
BAND = 128
ATTN_SCALE = HEAD_DIM ** -0.5
ROPE_DIM = HEAD_DIM // 4
ROPE_THETA = 500000.0
CONV_W = 4
CONV_DIM = 2 * B_HEADS * B_DK + B_HEADS * B_DV
N_EXPERTS = 32
TOP_K = 4
D_FF = D_MODEL
SWIGLU_LIMIT = 7.0
SWIGLU_ALPHA = 1.702
LN_EPS = 1e-5
NORM_EPS = 1e-6
N_IN = 3 * A_WIDTH + CONV_DIM + B_WIDTH + 2 * B_HEADS

LANES = 128
SUBLANES = 8
N_IN_PAD = -(-N_IN // LANES) * LANES
GDN_CHUNK = 64
GDN_CHUNK_SAMPLE = 16
MOE_TM = 512
MOE_TM_SAMPLE = 128
NEG = -1e30
VMEM_LIMIT = 56 * 1024 * 1024


def _dot(a, b):
    return jnp.dot(a, b, preferred_element_type=F32)


def _dot_nt(a, b):
    return lax.dot_general(a, b, (((1,), (1,)), ((), ())), preferred_element_type=F32)


def _dot_tn(a, b):
    return lax.dot_general(a, b, (((0,), (0,)), ((), ())), preferred_element_type=F32)


def _split2(x):
    hi = x.astype(BF16)
    lo = (x - hi.astype(F32)).astype(BF16)
    return hi, lo


def _split3(x):
    hi = x.astype(BF16)
    r = x - hi.astype(F32)
    mid = r.astype(BF16)
    lo = (r - mid.astype(F32)).astype(BF16)
    return hi, mid, lo


def _dot3(a, b, dot=_dot):
    ah, al = _split2(a)
    bh, bl = _split2(b)
    return dot(ah, bh) + (dot(ah, bl) + dot(al, bh))


def _dot_sel_rhs(a, sel_bf16):
    hi, lo = _split2(a)
    return _dot(hi, sel_bf16) + _dot(lo, sel_bf16)


def _dot_sel_lhs(sel_bf16, b):
    hi, lo = _split2(b)
    return _dot(sel_bf16, hi) + _dot(sel_bf16, lo)


def _sigmoid(x):
    return 1.0 / (1.0 + jnp.exp(-x))


def _softplus(x):
    return jnp.maximum(x, 0.0) + jnp.log(1.0 + jnp.exp(-jnp.abs(x)))


def _layer_norm(y, g, b):
    mu = jnp.mean(y, axis=-1, keepdims=True)
    yc = y - mu
    var = jnp.mean(yc * yc, axis=-1, keepdims=True)
    return yc * lax.rsqrt(var + LN_EPS) * g + b


def _ada_body(c_ref, w_ref, b_ref, o_ref):
    c = c_ref[...]
    s = c * _sigmoid(c)
    o_ref[...] = _dot3(s, w_ref[...]) + b_ref[...]


def _ada_mod(c, w_ada, b_ada):
    n = c.shape[0]
    n_pad = -(-n // SUBLANES) * SUBLANES
    c = jnp.pad(c, ((0, n_pad - n), (0, 0)))
    tn = 512
    out = pl.pallas_call(
        _ada_body,
        grid=(w_ada.shape[1] // tn,),
        in_specs=[pl.BlockSpec((n_pad, D_MODEL), lambda j: (0, 0)),
                  pl.BlockSpec((D_MODEL, tn), lambda j: (0, j)),
                  pl.BlockSpec((1, tn), lambda j: (0, j))],
        out_specs=pl.BlockSpec((n_pad, tn), lambda j: (0, j)),
        out_shape=jax.ShapeDtypeStruct((n_pad, w_ada.shape[1]), F32),
        name="ada_mod",
    )(c, w_ada, b_ada.reshape(1, -1))
    return out[:n]


def _rope_tables(pos):
    half = ROPE_DIM // 2
    inv = ROPE_THETA ** (-jnp.arange(half, dtype=F32) / half)
    ang = pos.astype(F32)[:, None] * inv[None, :]
    cos, sin = jnp.cos(ang), jnp.sin(ang)
    t = pos.shape[0]
    pad = jnp.zeros((t, HEAD_DIM - ROPE_DIM), F32)
    zero = jnp.zeros((t, half), F32)
    ct = jnp.concatenate([cos, cos, pad + 1.0], axis=1)
    s1 = jnp.concatenate([-sin, zero, pad], axis=1)
    s2 = jnp.concatenate([zero, sin, pad], axis=1)
    rep = LANES // HEAD_DIM
    return tuple(jnp.tile(a, (1, rep)) for a in (ct, s1, s2))


def _inproj_body(x_ref, sc_ref, sh_ref, ct_ref, s1_ref, s2_ref, w_ref,
                 q_ref, k_ref, v_ref, c_ref, z_ref, ab_ref):
    h = x_ref[...] * (1.0 + sc_ref[...]) + sh_ref[...]
    hb = h.astype(BF16)
    ct, s1, s2 = ct_ref[...], s1_ref[...], s2_ref[...]

    def rope(t):
        return t * ct + pltpu.roll(t, LANES - ROPE_DIM // 2, 1) * s1 + pltpu.roll(t, ROPE_DIM // 2, 1) * s2

    for j in range(A_WIDTH // LANES):
        cols = slice(LANES * j, LANES * (j + 1))
        q_ref[:, cols] = rope(_dot(hb, w_ref[:, LANES * j:LANES * (j + 1)])) * ATTN_SCALE
        k_ref[:, cols] = rope(_dot(hb, w_ref[:, A_WIDTH + LANES * j:A_WIDTH + LANES * (j + 1)]))
    o1 = 3 * A_WIDTH
    o2 = o1 + CONV_DIM
    o3 = o2 + B_WIDTH
    v_ref[...] = _dot(hb, w_ref[:, 2 * A_WIDTH:o1])
    c_ref[...] = _dot(hb, w_ref[:, o1:o2])
    z_ref[...] = _dot(hb, w_ref[:, o2:o3])
    ab_ref[...] = _dot(hb, w_ref[:, o3:N_IN_PAD])


def _in_proj(x, sc, sh, tables, w_pad, tm):
    b, t, _ = x.shape
    per_row = sc.shape[1] != 1
    mod_spec = (pl.BlockSpec((None, tm, D_MODEL), lambda bi, i: (bi, i, 0)) if per_row
                else pl.BlockSpec((None, 1, D_MODEL), lambda bi, i: (bi, 0, 0)))
    tab_spec = pl.BlockSpec((tm, LANES), lambda bi, i: (i, 0))

    def out(width):
        return (pl.BlockSpec((None, tm, width), lambda bi, i: (bi, i, 0)),
                jax.ShapeDtypeStruct((b, t, width), F32))

    outs = [out(A_WIDTH), out(A_WIDTH), out(A_WIDTH), out(CONV_DIM), out(B_WIDTH), out(LANES)]
    return pl.pallas_call(
        _inproj_body,
        grid=(b, t // tm),
        in_specs=[pl.BlockSpec((None, tm, D_MODEL), lambda bi, i: (bi, i, 0)), mod_spec, mod_spec,
                  tab_spec, tab_spec, tab_spec,
                  pl.BlockSpec((D_MODEL, N_IN_PAD), lambda bi, i: (0, 0))],
        out_specs=[o[0] for o in outs],
        out_shape=[o[1] for o in outs],
        compiler_params=pltpu.CompilerParams(vmem_limit_bytes=VMEM_LIMIT),
        name="in_proj",
    )(x, sc, sh, *tables, w_pad)


ATTN_TQ = 2048
ATTN_GROUP = 4


def _attn_body(q_ref, kc_ref, kp_ref, vc_ref, vp_ref, o_ref, kbuf, vbuf, acc, mst, lst, bias):
    tq = ATTN_TQ
    first_tile = pl.program_id(1) == 0
    kbuf[0:tq, :] = kp_ref[...]
    kbuf[tq:2 * tq, :] = kc_ref[...]
    vbuf[0:tq, :] = vp_ref[...]
    vbuf[tq:2 * tq, :] = vc_ref[...]

    qi = lax.broadcasted_iota(jnp.int32, (BAND, 2 * BAND), 0)
    ki = lax.broadcasted_iota(jnp.int32, (BAND, 2 * BAND), 1)
    dist = qi + BAND - ki
    bias[0] = jnp.where((dist >= 0) & (dist <= BAND), 0.0, NEG)
    bias[1] = jnp.where(ki >= BAND, 0.0, NEG)
    lo = lax.broadcasted_iota(jnp.int32, (BAND, LANES), 1) < HEAD_DIM

    for r, (window, d) in enumerate(DILATED_BRANCHES):
        assert window // d == BAND
        span = BAND * d

        def blk_group(grp, carry, r=r, d=d, span=span):
            q_idx, kb, vb, mask, qh = {}, {}, {}, {}, {}
            for u in range(ATTN_GROUP):
                idx = grp * ATTN_GROUP + u
                c = idx % d
                j = idx // d
                qs = j * span + c
                ks = tq + qs - span
                if d == 1:
                    qs = pl.multiple_of(qs, BAND)
                    ks = pl.multiple_of(ks, BAND)
                    q_idx[u], k_idx = pl.ds(qs, BAND), pl.ds(ks, 2 * BAND)
                else:
                    q_idx[u], k_idx = pl.ds(qs, BAND, stride=d), pl.ds(ks, 2 * BAND, stride=d)
                qb = q_ref[q_idx[u], :]
                kb[u] = kbuf[k_idx, :].astype(BF16)
                vb[u] = vbuf[k_idx, :].astype(BF16)
                no_prev = jnp.where(first_tile & (j == 0), 1.0, 0.0)
                mask[u] = bias[0] + bias[1] * no_prev
                qh[u, 0] = jnp.where(lo, qb, 0.0).astype(BF16)
                qh[u, 1] = jnp.where(lo, 0.0, qb).astype(BF16)
            chains = [(u, hh) for u in range(ATTN_GROUP) for hh in range(2)]
            s = {ch: _dot_nt(qh[ch], kb[ch[0]]) + mask[ch[0]] for ch in chains}
            m = {ch: jnp.max(s[ch], axis=-1, keepdims=True) for ch in chains}
            p = {ch: jnp.exp(s[ch] - m[ch]) for ch in chains}
            l = {ch: jnp.sum(p[ch], axis=-1, keepdims=True) for ch in chains}
            o = {ch: _dot(p[ch].astype(BF16), vb[ch[0]]) for ch in chains}
            for u in range(ATTN_GROUP):
                acc[r, q_idx[u], :] = jnp.where(lo, o[u, 0], o[u, 1])
                mst[r, q_idx[u], :] = jnp.where(lo, m[u, 0], m[u, 1])
                lst[r, q_idx[u], :] = jnp.where(lo, l[u, 0], l[u, 1])
            return carry

        lax.fori_loop(0, tq // (BAND * ATTN_GROUP), blk_group, 0)

    def combine(t, carry):
        rows = pl.ds(pl.multiple_of(t * BAND, BAND), BAND)
        ms = [mst[r, rows, :] for r in range(3)]
        m = jnp.maximum(jnp.maximum(ms[0], ms[1]), ms[2])
        ws = [jnp.exp(mr - m) for mr in ms]
        num = ws[0] * acc[0, rows, :] + ws[1] * acc[1, rows, :] + ws[2] * acc[2, rows, :]
        den = ws[0] * lst[0, rows, :] + ws[1] * lst[1, rows, :] + ws[2] * lst[2, rows, :]
        o_ref[rows, :] = (num / den).astype(o_ref.dtype)
        return carry

    lax.fori_loop(0, tq // BAND, combine, 0)


def _attn_prompt(q, k, v):
    b, t, _ = q.shape
    tq = ATTN_TQ
    assert t % tq == 0
    cur = pl.BlockSpec((None, tq, LANES), lambda bi, i, hp: (bi, i, hp))
    prev = pl.BlockSpec((None, tq, LANES), lambda bi, i, hp: (bi, jnp.maximum(i - 1, 0), hp))
    return pl.pallas_call(
        _attn_body,
        grid=(b, t // tq, A_WIDTH // LANES),
        in_specs=[cur, cur, prev, cur, prev],
        out_specs=cur,
        out_shape=jax.ShapeDtypeStruct((b, t, A_WIDTH), BF16),
        scratch_shapes=[pltpu.VMEM((2 * tq, LANES), F32), pltpu.VMEM((2 * tq, LANES), F32),
                        pltpu.VMEM((3, tq, LANES), F32), pltpu.VMEM((3, tq, LANES), F32),
                        pltpu.VMEM((3, tq, LANES), F32), pltpu.VMEM((2, BAND, 2 * BAND), F32)],
        compiler_params=pltpu.CompilerParams(vmem_limit_bytes=VMEM_LIMIT),
        name="attn_prompt",
    )(q, k, k, v, v)


def _branch_count(delta):
    cnt = jnp.zeros(delta.shape, F32)
    for window, d in DILATED_BRANCHES:
        hit = (delta >= 0) & (delta <= window) & ((delta & (d - 1)) == 0)
        cnt = cnt + jnp.where(hit, 1.0, 0.0)
    return cnt


def _attn_sample_body(q_ref, kn_ref, vn_ref, ck_hbm, cv_hbm, o_ref, ok_hbm, ov_hbm,
                      inb, outb, p_cache, p_new, placed, in_sem, out_sem, *, a_buf, t_new):
    b = pl.program_id(0)
    w = pl.program_id(1)
    nb = pl.num_programs(0)
    srcs = (ck_hbm, cv_hbm)
    dsts = (ok_hbm, ov_hbm)
    n_tiles = a_buf // LANES
    new0 = LANES - t_new

    def in_copy(ws, bi):
        return pltpu.make_async_copy(srcs[ws].at[bi], inb.at[ws], in_sem.at[ws])

    def out_copy(ws, bi):
        return pltpu.make_async_copy(outb.at[ws], dsts[ws].at[bi], out_sem.at[ws])

    lane = lax.broadcasted_iota(jnp.int32, (t_new, LANES), 1)
    qrow = lax.broadcasted_iota(jnp.int32, (t_new, LANES), 0)
    place = jnp.where(lane == new0 + qrow, 1.0, 0.0).astype(BF16)

    def place_new(x_ref):
        hi, mid, lo = _split3(x_ref[...])
        xt = _dot_tn(hi, place) + (_dot_tn(mid, place) + _dot_tn(lo, place))
        return xt.reshape(A_HEADS, HEAD_DIM, LANES)

    def shift_into(ws):
        keep = lax.broadcasted_iota(jnp.int32, (HEAD_DIM, LANES), 1) < new0

        def per_head(h, carry):
            prev = pltpu.roll(inb[ws, h, :, 0:LANES], new0, 1)
            for j in range(n_tiles):
                if j + 1 < n_tiles:
                    nxt = pltpu.roll(inb[ws, h, :, (j + 1) * LANES:(j + 2) * LANES], new0, 1)
                else:
                    nxt = placed[h]
                outb[ws, h, :, j * LANES:(j + 1) * LANES] = jnp.where(keep, prev, nxt)
                prev = nxt
            return carry

        lax.fori_loop(0, A_HEADS, per_head, 0)

    @pl.when(w == 0)
    def _():
        @pl.when(b == 0)
        def _():
            in_copy(0, 0).start()

        in_copy(1, b).start()
        placed[...] = place_new(kn_ref)
        in_copy(0, b).wait()

        pos = lax.broadcasted_iota(jnp.int32, (t_new, a_buf), 1)
        cnt = _branch_count(a_buf + lax.broadcasted_iota(jnp.int32, (t_new, a_buf), 0) - pos)
        cnt_new = jnp.where(lane >= new0, _branch_count(qrow - (lane - new0)), 0.0)
        q_all = q_ref[...]
        for h in range(A_HEADS):
            qh = q_all[:, h * HEAD_DIM:(h + 1) * HEAD_DIM].astype(BF16)
            s = jnp.where(cnt > 0.0, _dot(qh, inb[0, h].astype(BF16)), NEG)
            sn = jnp.where(cnt_new > 0.0, _dot(qh, placed[h].astype(BF16)), NEG)
            m = jnp.maximum(jnp.max(s, axis=-1, keepdims=True), jnp.max(sn, axis=-1, keepdims=True))
            p = cnt * jnp.exp(s - m)
            pn = cnt_new * jnp.exp(sn - m)
            inv_l = 1.0 / (jnp.sum(p, axis=-1, keepdims=True) + jnp.sum(pn, axis=-1, keepdims=True))
            p_cache[h] = p * inv_l
            p_new[h] = pn * inv_l

        @pl.when(b > 0)
        def _():
            out_copy(0, b - 1).wait()

        shift_into(0)
        out_copy(0, b).start()

    @pl.when(w == 1)
    def _():
        @pl.when(b + 1 < nb)
        def _():
            in_copy(0, b + 1).start()

        placed[...] = place_new(vn_ref)
        in_copy(1, b).wait()
        for h in range(A_HEADS):
            o_h = (_dot_nt(p_cache[h].astype(BF16), inb[1, h].astype(BF16))
                   + _dot_nt(p_new[h].astype(BF16), placed[h].astype(BF16)))
            o_ref[:, h * HEAD_DIM:(h + 1) * HEAD_DIM] = o_h

        @pl.when(b > 0)
        def _():
            out_copy(1, b - 1).wait()

        shift_into(1)
        out_copy(1, b).start()

        @pl.when(b == nb - 1)
        def _():
            out_copy(0, b).wait()
            out_copy(1, b).wait()


def _attn_sample(q, k_new, v_new, cache_kt, cache_vt):
    db, t_new, _ = q.shape
    a_buf = cache_kt.shape[-1]
    assert t_new == SUBLANES and a_buf == W_MAX and cache_kt.shape[1:3] == (A_HEADS, HEAD_DIM)
    row = pl.BlockSpec((None, t_new, A_WIDTH), lambda bi, wi: (bi, 0, 0))
    hbm = pl.BlockSpec(memory_space=pl.ANY)
    body = functools.partial(_attn_sample_body, a_buf=a_buf, t_new=t_new)
    return pl.pallas_call(
        body,
        grid=(db, 2),
        in_specs=[row, row, row, hbm, hbm],
        out_specs=[row, hbm, hbm],
        out_shape=[jax.ShapeDtypeStruct((db, t_new, A_WIDTH), F32),
                   jax.ShapeDtypeStruct(cache_kt.shape, F32), jax.ShapeDtypeStruct(cache_vt.shape, F32)],
        scratch_shapes=[pltpu.VMEM((2, A_HEADS, HEAD_DIM, a_buf), F32), pltpu.VMEM((2, A_HEADS, HEAD_DIM, a_buf), F32),
                        pltpu.VMEM((A_HEADS, t_new, a_buf), F32), pltpu.VMEM((A_HEADS, t_new, LANES), F32),
                        pltpu.VMEM((A_HEADS, HEAD_DIM, LANES), F32),
                        pltpu.SemaphoreType.DMA((2,)), pltpu.SemaphoreType.DMA((2,))],
        compiler_params=pltpu.CompilerParams(vmem_limit_bytes=VMEM_LIMIT,
                                             dimension_semantics=("arbitrary", "arbitrary")),
        name="attn_sample",
    )(q, k_new, v_new, cache_kt, cache_vt)


def _gdn_body(xc_ref, z_ref, ab_ref, cw_ref, cs_ref, s0_ref, al_ref, dtb_ref, nw_ref,
              o_ref, nc_ref, ns_ref,
              xbuf, state, qs, ks, vs, bs, gs, *, bb, tc, c, t_valid, unroll):
    g = pl.program_id(1)
    hd = B_DK
    nh = B_HEADS
    width = nh * hd

    @pl.when(g == 0)
    def _():
        xbuf[:, 0:SUBLANES, :] = cs_ref[...]
        state[...] = s0_ref[...]

    grp_r = lax.broadcasted_iota(jnp.int32, (width, width), 0) // hd
    grp_c = lax.broadcasted_iota(jnp.int32, (width, width), 1) // hd
    head_sum = jnp.where(grp_r == grp_c, 1.0, 0.0).astype(BF16)
    src = lax.broadcasted_iota(jnp.int32, (LANES, width), 0)
    dst = lax.broadcasted_iota(jnp.int32, (LANES, width), 1) // hd
    pick_a = jnp.where(src == dst, 1.0, 0.0).astype(BF16)
    pick_b = jnp.where(src == dst + nh, 1.0, 0.0).astype(BF16)

    for bi in range(bb):
        xbuf[bi, SUBLANES:SUBLANES + tc, :] = xc_ref[bi]
        y = xbuf[bi, pl.ds(SUBLANES - (CONV_W - 1), tc), :] * cw_ref[0:1, :]
        for j in range(1, CONV_W):
            y = y + xbuf[bi, pl.ds(SUBLANES - (CONV_W - 1) + j, tc), :] * cw_ref[j:j + 1, :]
        y = y * _sigmoid(y)
        tail = xbuf[bi, t_valid:t_valid + SUBLANES, :]
        xbuf[bi, 0:SUBLANES, :] = tail
        nc_ref[bi] = tail

        q = y[:, 0:width]
        k = y[:, width:2 * width]
        qs[bi] = q * lax.rsqrt(_dot_sel_rhs(q * q, head_sum) + NORM_EPS) * (hd ** -0.5)
        ks[bi] = k * lax.rsqrt(_dot_sel_rhs(k * k, head_sum) + NORM_EPS)
        vs[bi] = y[:, 2 * width:3 * width]
        ab = ab_ref[bi]
        beta = _sigmoid(_dot_sel_rhs(ab, pick_b))
        gate = -jnp.exp(al_ref[...]) * _softplus(_dot_sel_rhs(ab, pick_a) + dtb_ref[...])
        if t_valid < tc:
            live = lax.broadcasted_iota(jnp.int32, (tc, width), 0) < t_valid
            beta = jnp.where(live, beta, 0.0)
            gate = jnp.where(live, gate, 0.0)
        bs[bi] = beta
        gs[bi] = gate

    ri = lax.broadcasted_iota(jnp.int32, (c, c), 0)
    ci = lax.broadcasted_iota(jnp.int32, (c, c), 1)
    tri_incl = ri >= ci
    tri_strict = ri > ci
    tri_incl_b = jnp.where(tri_incl, 1.0, 0.0).astype(BF16)
    eye = jnp.where(ri == ci, 1.0, 0.0)
    ones_b = jnp.ones((c, c), BF16)
    n_sq = int(math.log2(c)) - 2

    n_sub = unroll

    def chunk_group(cg, carry):
        seqs = [(s, bi) for s in range(n_sub) for bi in range(bb)]
        pre = {}
        for s, bi in seqs:
            rows = pl.ds(pl.multiple_of((cg * n_sub + s) * c, c), c)
            qc, kc, vc, bc = qs[bi, rows, :], ks[bi, rows, :], vs[bi, rows, :], bs[bi, rows, :]
            gcum = _dot_sel_lhs(tri_incl_b, gs[bi, rows, :])
            glast = gcum[c - 1:c, :]
            eg = jnp.exp(gcum)
            kb = kc * bc
            pre[s, bi] = dict(rows=rows, gcum=gcum, glast=glast, kcb=kc.astype(BF16), kbb=kb.astype(BF16),
                              qcb=qc.astype(BF16), vbb=(vc * bc).astype(BF16), wkb=(kb * eg).astype(BF16),
                              qgb=(qc * eg).astype(BF16), kdb=(kc * jnp.exp(glast - gcum)).astype(BF16))
        chains = [(s, bi, h) for s, bi in seqs for h in range(nh)]
        sl = lambda h: slice(h * hd, (h + 1) * hd)
        g_col = {ch: pre[ch[:2]]['gcum'][:, ch[2] * hd:ch[2] * hd + c] for ch in chains}
        g_row = {ch: _dot_sel_lhs(ones_b, eye * g_col[ch]) for ch in chains}
        decay = {ch: jnp.where(tri_incl, jnp.exp(g_col[ch] - g_row[ch]), 0.0) for ch in chains}
        kk = {ch: _dot_nt(pre[ch[:2]]['kbb'][:, sl(ch[2])], pre[ch[:2]]['kcb'][:, sl(ch[2])]) for ch in chains}
        qk = {ch: _dot_nt(pre[ch[:2]]['qcb'][:, sl(ch[2])], pre[ch[:2]]['kcb'][:, sl(ch[2])]) for ch in chains}
        attn = {ch: (qk[ch] * decay[ch]).astype(BF16) for ch in chains}
        neg = {ch: jnp.where(tri_strict, -(kk[ch] * decay[ch]), 0.0) for ch in chains}
        tinv = {ch: eye + neg[ch] for ch in chains}
        negb = {ch: neg[ch].astype(BF16) for ch in chains}
        pw = {ch: _dot(negb[ch], negb[ch]) for ch in chains}
        tinv = {ch: tinv[ch] + _dot(tinv[ch].astype(BF16), pw[ch].astype(BF16)) for ch in chains}
        for _ in range(n_sq):
            pwb = {ch: pw[ch].astype(BF16) for ch in chains}
            pw = {ch: _dot(pwb[ch], pwb[ch]) for ch in chains}
            tinv = {ch: tinv[ch] + _dot(tinv[ch].astype(BF16), pw[ch].astype(BF16)) for ch in chains}
        tb = {ch: tinv[ch].astype(BF16) for ch in chains}
        u = {ch: _dot(tb[ch], pre[ch[:2]]['vbb'][:, sl(ch[2])]) for ch in chains}
        w = {ch: _dot(tb[ch], pre[ch[:2]]['wkb'][:, sl(ch[2])]).astype(BF16) for ch in chains}
        for s in range(n_sub):
            live = [ch for ch in chains if ch[0] == s]
            s_old = {ch: state[ch[1], ch[2]] for ch in live}
            s_b = {ch: s_old[ch].astype(BF16) for ch in live}
            ws = {ch: _dot(w[ch], s_b[ch]) for ch in live}
            qs_ = {ch: _dot(pre[ch[:2]]['qgb'][:, sl(ch[2])], s_b[ch]) for ch in live}
            v_new = {ch: (u[ch] - ws[ch]).astype(BF16) for ch in live}
            av = {ch: _dot(attn[ch], v_new[ch]) for ch in live}
            kv = {ch: _dot_tn(pre[ch[:2]]['kdb'][:, sl(ch[2])], v_new[ch]) for ch in live}
            for ch in live:
                state[ch[1], ch[2]] = s_old[ch] * jnp.exp(pre[ch[:2]]['glast'][:, sl(ch[2])]) + kv[ch]
            for bi in range(bb):
                outs = []
                for h in range(nh):
                    o_h = qs_[s, bi, h] + av[s, bi, h]
                    outs.append(o_h * lax.rsqrt(jnp.mean(o_h * o_h, axis=-1, keepdims=True) + NORM_EPS))
                rows = pre[s, bi]['rows']
                zc = z_ref[bi, rows, :]
                o_ref[bi, rows, :] = (jnp.concatenate(outs, axis=1) * nw_ref[...]
                                      * (zc * _sigmoid(zc))).astype(o_ref.dtype)
        return carry

    lax.fori_loop(0, tc // (c * n_sub), chunk_group, 0)
    ns_ref[...] = state[...]


def _gdn(conv_in, z, ab, conv_w, conv_state, ssm0, a_log, dt_bias, norm_w, *, chunk, t_valid=None):
    b, t, _ = conv_in.shape
    c = chunk
    bb = 2
    tc = min(t, 4 * c)
    assert t % tc == 0 and tc % c == 0 and b % bb == 0 and c <= B_DK
    t_valid = t if t_valid is None else t_valid
    assert t_valid == t or t == tc
    n_chunks = tc // c
    rep = lambda a: jnp.repeat(a.astype(F32), B_DK).reshape(1, B_WIDTH)
    blk = lambda w: pl.BlockSpec((bb, tc, w), lambda bi, gi: (bi, gi, 0))
    const2 = lambda shape: pl.BlockSpec(shape, lambda bi, gi: (0, 0))
    body = functools.partial(_gdn_body, bb=bb, tc=tc, c=c, t_valid=min(t_valid, tc),
                             unroll=2 if n_chunks % 2 == 0 else 1)
    tail_spec = pl.BlockSpec((bb, SUBLANES, CONV_DIM), lambda bi, gi: (bi, 0, 0))
    state_spec = pl.BlockSpec((bb, B_HEADS, B_DK, B_DV), lambda bi, gi: (bi, 0, 0, 0))
    conv_tail = jnp.pad(conv_state, ((0, 0), (SUBLANES - (CONV_W - 1), 0), (0, 0)))
    o, new_tail, new_state = pl.pallas_call(
        body,
        grid=(b // bb, t // tc),
        in_specs=[blk(CONV_DIM), blk(B_WIDTH), blk(LANES), const2((CONV_W, CONV_DIM)),
                  tail_spec, state_spec,
                  const2((1, B_WIDTH)), const2((1, B_WIDTH)), const2((1, B_WIDTH))],
        out_specs=[blk(B_WIDTH), tail_spec, state_spec],
        out_shape=[jax.ShapeDtypeStruct((b, t, B_WIDTH), BF16),
                   jax.ShapeDtypeStruct((b, SUBLANES, CONV_DIM), F32),
                   jax.ShapeDtypeStruct((b, B_HEADS, B_DK, B_DV), F32)],
        scratch_shapes=[pltpu.VMEM((bb, tc + SUBLANES, CONV_DIM), F32),
                        pltpu.VMEM((bb, B_HEADS, B_DK, B_DV), F32)]
        + [pltpu.VMEM((bb, tc, B_WIDTH), F32)] * 5,
        compiler_params=pltpu.CompilerParams(vmem_limit_bytes=VMEM_LIMIT,
                                             dimension_semantics=("arbitrary", "arbitrary")),
        name="gdn",
    )(conv_in, z, ab, conv_w, conv_tail, ssm0, rep(a_log), rep(dt_bias),
      jnp.tile(norm_w.astype(F32), B_HEADS).reshape(1, B_WIDTH))
    return o, new_tail[:, SUBLANES - (CONV_W - 1):], new_state


def _outproj_body(oa_ref, ob_ref, x_ref, g1_ref, sc2_ref, sh2_ref, w_ref, lg_ref, lb_ref, rw_ref, rb_ref,
                  x1_ref, h2_ref, ti_ref, tw_ref, cnt_ref, *, dn_alpha):
    mix = _dot(oa_ref[...], w_ref[0:A_WIDTH, :]) + _dot(ob_ref[...], w_ref[A_WIDTH:A_WIDTH + B_WIDTH, :])
    x1 = _layer_norm(dn_alpha * x_ref[...] + (1.0 + g1_ref[...]) * mix, lg_ref[...], lb_ref[...])
    x1_ref[...] = x1
    h2 = x1 * (1.0 + sc2_ref[...]) + sh2_ref[...]
    h2_ref[...] = h2.astype(BF16)
    logits = _dot3(h2, rw_ref[...]) + rb_ref[...]
    lane = lax.broadcasted_iota(jnp.int32, logits.shape, 1)
    ti = jnp.zeros(logits.shape, jnp.int32)
    tv = jnp.zeros(logits.shape, F32)
    picked = jnp.zeros(logits.shape, F32)
    vals = []
    for kk in range(TOP_K):
        m = jnp.max(logits, axis=-1, keepdims=True)
        idx = jnp.min(jnp.where(logits == m, lane.astype(F32), float(LANES)), axis=-1, keepdims=True).astype(jnp.int32)
        vals.append(m)
        ti = jnp.where(lane == kk, idx, ti)
        picked = jnp.where(lane == idx, 1.0, picked)
        logits = jnp.where(lane == idx, NEG, logits)
    es = [jnp.exp(v - vals[0]) for v in vals]
    den = es[0] + es[1] + es[2] + es[3]
    for kk in range(TOP_K):
        tv = jnp.where(lane == kk, es[kk] / den, tv)
    ti_ref[...] = ti
    tw_ref[...] = tv
    per_expert = jnp.sum(picked, axis=0, keepdims=True)
    row = lax.broadcasted_iota(jnp.int32, (SUBLANES, LANES), 0)
    cnt_ref[...] = jnp.where(row == 0, per_expert, 0.0)


def _out_proj(oa, ob, x, g1, sc2, sh2, w_out_b, ln_g, ln_b, rw_pad, rb_pad, tm, dn_alpha):
    b, t, _ = x.shape
    per_row = g1.shape[1] != 1
    mod_spec = (pl.BlockSpec((None, tm, D_MODEL), lambda bi, i: (bi, i, 0)) if per_row
                else pl.BlockSpec((None, 1, D_MODEL), lambda bi, i: (bi, 0, 0)))
    blk = lambda w: pl.BlockSpec((None, tm, w), lambda bi, i: (bi, i, 0))
    const2 = lambda shape: pl.BlockSpec(shape, lambda bi, i: (0, 0))
    return pl.pallas_call(
        functools.partial(_outproj_body, dn_alpha=dn_alpha),
        grid=(b, t // tm),
        in_specs=[blk(A_WIDTH), blk(B_WIDTH), blk(D_MODEL), mod_spec, mod_spec, mod_spec,
                  const2((D_MODEL, D_MODEL)), const2((1, D_MODEL)), const2((1, D_MODEL)),
                  const2((D_MODEL, LANES)), const2((1, LANES))],
        out_specs=[blk(D_MODEL), blk(D_MODEL), blk(LANES), blk(LANES),
                   pl.BlockSpec((None, SUBLANES, LANES), lambda bi, i: (bi, i, 0))],
        out_shape=[jax.ShapeDtypeStruct((b, t, D_MODEL), F32), jax.ShapeDtypeStruct((b, t, D_MODEL), BF16),
                   jax.ShapeDtypeStruct((b, t, LANES), jnp.int32), jax.ShapeDtypeStruct((b, t, LANES), F32),
                   jax.ShapeDtypeStruct((b, t // tm * SUBLANES, LANES), F32)],
        compiler_params=pltpu.CompilerParams(vmem_limit_bytes=VMEM_LIMIT),
        name="out_proj",
    )(oa, ob, x, g1, sc2, sh2, w_out_b, ln_g, ln_b, rw_pad, rb_pad)


def _moe_body(be_ref, nv_ref, x_ref, wgu_ref, bgu_ref, wd_ref, bd_ref, o_ref, wgu_b, wd_b):
    i = pl.program_id(0)

    @pl.when((i == 0) | (be_ref[i] != be_ref[jnp.maximum(i - 1, 0)]))
    def _():
        rows = 128

        def cast_rows(t, carry):
            r = pl.ds(pl.multiple_of(t * rows, rows), rows)
            wgu_b[r, :] = wgu_ref[r, :].astype(BF16)
            wd_b[r, :] = wd_ref[r, :].astype(BF16)
            return carry

        lax.fori_loop(0, D_MODEL // rows, cast_rows, 0)

    @pl.when(i < nv_ref[0])
    def _():
        hcat = _dot(x_ref[...], wgu_b[...]) + bgu_ref[...]
        gate = jnp.minimum(hcat[:, :D_FF], SWIGLU_LIMIT)
        up = jnp.clip(hcat[:, D_FF:], -SWIGLU_LIMIT, SWIGLU_LIMIT)
        glu = gate * _sigmoid(SWIGLU_ALPHA * gate)
        act = (glu * (up + 1.0)).astype(BF16)
        o_ref[...] = _dot(act, wd_b[...]) + bd_ref[...]

    @pl.when(i >= nv_ref[0])
    def _():
        o_ref[...] = jnp.zeros(o_ref.shape, o_ref.dtype)


def _moe_experts(xs, blk_e, n_valid, w_gu, b_gu, w_down, b_down, tm):
    cap = xs.shape[0]
    assert D_FF == D_MODEL
    grid_spec = pltpu.PrefetchScalarGridSpec(
        num_scalar_prefetch=2,
        grid=(cap // tm,),
        in_specs=[pl.BlockSpec((tm, D_MODEL), lambda i, be, nv: (i, 0)),
                  pl.BlockSpec((None, D_MODEL, 2 * D_FF), lambda i, be, nv: (be[i], 0, 0)),
                  pl.BlockSpec((None, 1, 2 * D_FF), lambda i, be, nv: (be[i], 0, 0)),
                  pl.BlockSpec((None, D_FF, D_MODEL), lambda i, be, nv: (be[i], 0, 0)),
                  pl.BlockSpec((None, 1, D_MODEL), lambda i, be, nv: (be[i], 0, 0))],
        out_specs=pl.BlockSpec((tm, D_MODEL), lambda i, be, nv: (i, 0)),
        scratch_shapes=[pltpu.VMEM((D_MODEL, 2 * D_FF), BF16), pltpu.VMEM((D_FF, D_MODEL), BF16)],
    )
    return pl.pallas_call(
        _moe_body,
        grid_spec=grid_spec,
        out_shape=jax.ShapeDtypeStruct((cap, D_MODEL), F32),
        compiler_params=pltpu.CompilerParams(vmem_limit_bytes=VMEM_LIMIT,
                                             dimension_semantics=("arbitrary",)),
        name="moe_experts",
    )(blk_e, n_valid, xs, w_gu, b_gu.reshape(N_EXPERTS, 1, 2 * D_FF), w_down,
      b_down.reshape(N_EXPERTS, 1, D_MODEL))


def _take_rows(x, idx):
    return x.at[idx].get(mode="promise_in_bounds")


def _route(eid, counts, n_tok, tm):
    n_assign = n_tok * TOP_K
    order = jnp.argsort(eid, stable=True).astype(jnp.int32)
    start = jnp.cumsum(counts) - counts
    padded = (counts + tm - 1) // tm * tm
    pstart = jnp.cumsum(padded) - padded
    pend = pstart + padded
    n_blocks = -(-(n_assign + N_EXPERTS * (tm - 1)) // tm)
    blk_first = jnp.arange(n_blocks, dtype=jnp.int32) * tm
    blk_e = jnp.minimum(jnp.sum((pend[None, :] <= blk_first[:, None]).astype(jnp.int32), axis=1), N_EXPERTS - 1)
    within = jnp.arange(tm, dtype=jnp.int32)[None, :]
    rank = (blk_first - pstart[blk_e])[:, None] + within
    live = (rank < counts[blk_e][:, None]) & (blk_first < pend[N_EXPERTS - 1])[:, None]
    src = jnp.clip(start[blk_e][:, None] + rank, 0, n_assign - 1)
    assign = _take_rows(order, src.reshape(-1))
    row_tok = jnp.where(live.reshape(-1), assign % n_tok, 0)
    shift = pstart - start
    step = jnp.concatenate([shift[:1], shift[1:] - shift[:-1]])
    pos = jnp.arange(n_assign, dtype=jnp.int32)
    dest_sorted = pos + jnp.sum(jnp.where(pos[:, None] >= start[None, :], step[None, :], 0), axis=1)
    inv = jnp.argsort(order).astype(jnp.int32)
    dest = _take_rows(dest_sorted, inv).reshape(TOP_K, n_tok)
    n_valid = (pend[N_EXPERTS - 1] // tm).astype(jnp.int32).reshape(1)
    return row_tok, blk_e.astype(jnp.int32), n_valid, dest


def _final_body(x1_ref, m0_ref, m1_ref, m2_ref, m3_ref, tw_ref, g2_ref, lg_ref, lb_ref, o_ref, *, dn_alpha):
    tw = tw_ref[...]
    moe = ((m0_ref[...] * tw[:, 0:1] + m1_ref[...] * tw[:, 1:2])
           + (m2_ref[...] * tw[:, 2:3] + m3_ref[...] * tw[:, 3:4]))
    y = dn_alpha * x1_ref[...] + (1.0 + g2_ref[...]) * moe
    o_ref[...] = _layer_norm(y, lg_ref[...], lb_ref[...])


def _final_norm(x1, moe_parts, top_w, g2, ln_g, ln_b, tm, dn_alpha):
    b, t, _ = x1.shape
    per_row = g2.shape[1] != 1
    mod_spec = (pl.BlockSpec((None, tm, D_MODEL), lambda bi, i: (bi, i, 0)) if per_row
                else pl.BlockSpec((None, 1, D_MODEL), lambda bi, i: (bi, 0, 0)))
    blk = pl.BlockSpec((None, tm, D_MODEL), lambda bi, i: (bi, i, 0))
    const2 = pl.BlockSpec((1, D_MODEL), lambda bi, i: (0, 0))
    assert len(moe_parts) == TOP_K
    return pl.pallas_call(
        functools.partial(_final_body, dn_alpha=dn_alpha),
        grid=(b, t // tm),
        in_specs=[blk] * (1 + TOP_K) + [pl.BlockSpec((None, tm, LANES), lambda bi, i: (bi, i, 0)),
                                        mod_spec, const2, const2],
        out_specs=blk,
        out_shape=jax.ShapeDtypeStruct((b, t, D_MODEL), F32),
        compiler_params=pltpu.CompilerParams(vmem_limit_bytes=VMEM_LIMIT),
        name="final_norm",
    )(x1, *moe_parts, top_w, g2, ln_g, ln_b)


def _layer(x_prompt, x_sample, c_prompt, c_sample, cache_k, cache_v, conv_state, ssm_state, past_len, depth,
           w_ada, b_ada, w_in, conv_w, a_log, dt_bias, gdn_norm_w, w_out, ln1_g, ln1_b,
           router_w, router_b, w_gu, b_gu, w_down, b_down, ln2_g, ln2_b):
    bp, tp, _ = x_prompt.shape
    db, ts, _ = x_sample.shape
    dn_alpha = (2.0 * depth) ** 0.25
    n_s = db * ts
    tm = 512
    row2 = lambda a: a.reshape(1, -1)

    mod = _ada_mod(jnp.concatenate([c_prompt, c_sample], axis=0), w_ada, b_ada)
    mod_p = [m[:, None, :] for m in jnp.split(mod[:bp], 6, axis=-1)]
    mod_s = [jnp.repeat(m, ts, axis=0)[None] for m in jnp.split(mod[bp:], 6, axis=-1)]

    w_in_b = jnp.pad(w_in, ((0, 0), (0, N_IN_PAD - N_IN))).astype(BF16)
    tab_p = _rope_tables(jnp.arange(tp, dtype=jnp.int32))
    tab_s = _rope_tables(jnp.tile(past_len + jnp.arange(ts, dtype=jnp.int32), db))
    xs_flat = x_sample.reshape(1, n_s, D_MODEL)

    qp, kp, vp, cin_p, z_p, ab_p = _in_proj(x_prompt, mod_p[1], mod_p[0], tab_p, w_in_b, tm)
    qs, ks, vs, cin_s, z_s, ab_s = _in_proj(xs_flat, mod_s[1], mod_s[0], tab_s, w_in_b, tm)

    oa_p = _attn_prompt(qp, kp, vp)
    keep = min(W_MAX, tp)
    new_k_p = kp[:, tp - keep:].reshape(bp, keep, A_HEADS, HEAD_DIM)
    new_v_p = vp[:, tp - keep:].reshape(bp, keep, A_HEADS, HEAD_DIM)
    oa_s, new_k_s, new_v_s = _attn_sample(
        qs.reshape(db, ts, A_WIDTH), ks.reshape(db, ts, A_WIDTH), vs.reshape(db, ts, A_WIDTH),
        cache_k.transpose(0, 2, 3, 1), cache_v.transpose(0, 2, 3, 1))
    new_k_s = new_k_s.transpose(0, 3, 1, 2)
    new_v_s = new_v_s.transpose(0, 3, 1, 2)

    ob_p, new_conv_p, new_ssm_p = _gdn(
        cin_p, z_p, ab_p, conv_w, jnp.zeros((bp, CONV_W - 1, CONV_DIM), F32),
        jnp.zeros((bp, B_HEADS, B_DK, B_DV), F32), a_log, dt_bias, gdn_norm_w, chunk=GDN_CHUNK)
    pad_t = lambda a: jnp.pad(a.reshape(db, ts, a.shape[-1]), ((0, 0), (0, GDN_CHUNK_SAMPLE - ts), (0, 0)))
    ob_s, new_conv_s, new_ssm_s = _gdn(pad_t(cin_s), pad_t(z_s), pad_t(ab_s), conv_w, conv_state, ssm_state,
                                       a_log, dt_bias, gdn_norm_w, chunk=GDN_CHUNK_SAMPLE, t_valid=ts)
    ob_s = ob_s[:, :ts]

    w_out_b = w_out.astype(BF16)
    rw_pad = jnp.pad(router_w, ((0, 0), (0, LANES - N_EXPERTS)))
    rb_pad = jnp.pad(router_b, (0, LANES - N_EXPERTS), constant_values=NEG).reshape(1, LANES)
    x1_p, h2_p, ti_p, tw_p, cnt_p = _out_proj(oa_p, ob_p, x_prompt, mod_p[2], mod_p[4], mod_p[3], w_out_b,
                                              row2(ln1_g), row2(ln1_b), rw_pad, rb_pad, tm, dn_alpha)
    x1_s, h2_s, ti_s, tw_s, cnt_s = _out_proj(oa_s.astype(BF16).reshape(1, n_s, A_WIDTH),
                                              ob_s.reshape(1, n_s, B_WIDTH), xs_flat,
                                              mod_s[2], mod_s[4], mod_s[3], w_out_b,
                                              row2(ln1_g), row2(ln1_b), rw_pad, rb_pad, tm, dn_alpha)

    def moe(h2, top_i, cnt, tm_rows):
        n_tok = h2.shape[0] * h2.shape[1]
        counts = cnt.sum(axis=(0, 1))[:N_EXPERTS].astype(jnp.int32)
        eid = top_i.reshape(n_tok, LANES)[:, :TOP_K].T.reshape(-1)
        row_tok, blk_e, n_valid, dest = _route(eid, counts, n_tok, tm_rows)
        xs_sorted = _take_rows(h2.reshape(n_tok, D_MODEL), row_tok)
        ys = _moe_experts(xs_sorted, blk_e, n_valid, w_gu, b_gu, w_down, b_down, tm_rows)
        return [_take_rows(ys, dest[kk]).reshape(h2.shape) for kk in range(TOP_K)]

    parts_p = moe(h2_p, ti_p, cnt_p, MOE_TM)
    parts_s = moe(h2_s, ti_s, cnt_s, MOE_TM_SAMPLE)

    y_p = _final_norm(x1_p, parts_p, tw_p, mod_p[5], row2(ln2_g), row2(ln2_b), tm, dn_alpha)
    y_s = _final_norm(x1_s, parts_s, tw_s, mod_s[5], row2(ln2_g), row2(ln2_b), tm, dn_alpha)
    return (y_p, y_s.reshape(db, ts, D_MODEL), new_k_p, new_v_p, new_conv_p, new_ssm_p,
            new_k_s, new_v_s, new_conv_s, new_ssm_s)


def kernel(x_prompt, x_sample, c_prompt, c_sample, cache_a_k, cache_a_v, state_b_conv, state_b_ssm, w_ada, b_ada, w_in, conv_w, a_log, dt_bias, gdn_norm_w, w_out, ln1_g, ln1_b, router_w, router_b, w_gu, b_gu, w_down, b_down, ln2_g, ln2_b):
    depth = w_ada.shape[0]
    assert depth == 1
    past_len = 8192
    outs = _layer(x_prompt, x_sample, c_prompt, c_sample, cache_a_k[0], cache_a_v[0], state_b_conv[0],
                  state_b_ssm[0], past_len, depth,
                  w_ada[0], b_ada[0], w_in[0], conv_w[0], a_log[0], dt_bias[0], gdn_norm_w[0], w_out[0],
                  ln1_g[0], ln1_b[0], router_w[0], router_b[0], w_gu[0], b_gu[0], w_down[0], b_down[0],
                  ln2_g[0], ln2_b[0])
    y_p, y_s = outs[0], outs[1]
    return (y_p, y_s) + tuple(o[None] for o in outs[2:])
```
